```python
import jax, jax.numpy as jnp
from jax import lax
import numpy as np


D_MODEL = 1024
BATCH = 4
SEQ = 4096
DEPTH = 1
DEC_BATCH = 32
DEC_SEQ = 8
PAST_LEN = 8192
PAGE_SIZE = 128

HEAD_DIM = 64
N_HEADS_A = 8
N_HEADS_B = 8
WIDTH_A = N_HEADS_A * HEAD_DIM
WIDTH_B = N_HEADS_B * HEAD_DIM
N_IDX_HEADS = 8
IDX_DIM = 64
TOPK_KEYS = 256
MOBA_BLOCK = 256
MOBA_TOPK = 3
D_FF = -(-(8 * D_MODEL) // (3 * 256)) * 256
ROPE_THETA = 10000.0
RMS_EPS = 1e-6
Q_BLOCK = 128
NEG = -1e30
IN_SPLITS = (WIDTH_A, WIDTH_A, WIDTH_A, N_IDX_HEADS * IDX_DIM, IDX_DIM, N_IDX_HEADS,
             WIDTH_B, WIDTH_B, WIDTH_B, D_MODEL, D_MODEL)
D_IN = 3 * WIDTH_A + N_IDX_HEADS * IDX_DIM + IDX_DIM + N_IDX_HEADS + 3 * WIDTH_B + 2 * D_MODEL

kernel_name = "hybrid_dsa_moba_decoder_step"


def rms_norm(x, g):
    xf = x.astype(jnp.float32)
    y = xf * lax.rsqrt(jnp.mean(xf * xf, axis=-1, keepdims=True) + RMS_EPS)
    return (y * g.astype(jnp.float32)).astype(x.dtype)


def rope(x, pos):
    d = x.shape[-1]
    inv_freq = ROPE_THETA ** (-jnp.arange(0, d, 2, dtype=jnp.float32) / d)
    ang = pos.astype(jnp.float32)[:, None] * inv_freq[None, :]
    cos = jnp.cos(ang)[:, None, :]
    sin = jnp.sin(ang)[:, None, :]
    xf = x.astype(jnp.float32)
    x1, x2 = xf[..., : d // 2], xf[..., d // 2:]
    return jnp.concatenate([x1 * cos - x2 * sin, x2 * cos + x1 * sin], axis=-1).astype(x.dtype)


def mixer_inputs(xn, w_in, g_idx_k, pos):
    bn, s, _ = xn.shape
    h = jnp.einsum("bsd,de->bse", xn, w_in)
    offsets = [int(v) for v in np.cumsum(IN_SPLITS)[:-1]]
    qa, ka, va, qi, ki, wi, qb, kb, vb, ga, gb = jnp.split(h, offsets, axis=-1)
    heads = lambda t, nh, hd: t.reshape(bn, s, nh, hd)
    qa = rope(heads(qa, N_HEADS_A, HEAD_DIM), pos)
    ka = rope(heads(ka, N_HEADS_A, HEAD_DIM), pos)
    va = heads(va, N_HEADS_A, HEAD_DIM)
    qi = rope(heads(qi, N_IDX_HEADS, IDX_DIM), pos)
    ki = rope(rms_norm(ki, g_idx_k)[:, :, None, :], pos)[:, :, 0, :]
    wi = wi * (N_IDX_HEADS ** -0.5)
    qb = rope(heads(qb, N_HEADS_B, HEAD_DIM), pos)
    kb = rope(heads(kb, N_HEADS_B, HEAD_DIM), pos)
    vb = heads(vb, N_HEADS_B, HEAD_DIM)
    return qa, ka, va, qi, ki, wi, qb, kb, vb, ga, gb


def dsa_attend(q, qi, wi, q_pos, ki_all, fetch_kv, k_sel):
    L = ki_all.shape[1]
    s = jnp.einsum("bqhd,bsd->bqhs", qi, ki_all).astype(jnp.float32) * (IDX_DIM ** -0.5)
    score = jnp.einsum("bqhs,bqh->bqs", jax.nn.relu(s), wi.astype(jnp.float32))
    admissible = jnp.arange(L)[None, :] <= q_pos[:, None]
    score = jnp.where(admissible[None], score, NEG)
    _, idx = lax.top_k(score, k_sel)
    valid = idx <= q_pos[None, :, None]
    k_g, v_g = fetch_kv(idx)
    logits = jnp.einsum("bqhd,bqkhd->bqhk", q, k_g).astype(jnp.float32) * (HEAD_DIM ** -0.5)
    logits = jnp.where(valid[:, :, None, :], logits, NEG)
    p = jax.nn.softmax(logits, axis=-1).astype(v_g.dtype)
    return jnp.einsum("bqhk,bqkhd->bqhd", p, v_g)


def to_blocks(t):
    L = t.shape[1]
    nb = -(-L // MOBA_BLOCK)
    tp = jnp.pad(t, ((0, 0), (0, nb * MOBA_BLOCK - L), (0, 0), (0, 0)))
    return tp.reshape(t.shape[0], nb, MOBA_BLOCK, t.shape[2], t.shape[3])


def block_means(blk):
    return blk.astype(jnp.float32).mean(axis=2).astype(blk.dtype)


def moba_attend(q, q_pos, k_means, fetch_sel, own_k, own_v, own_pos):
    bq, nq, nh, hd = q.shape
    nb = k_means.shape[1]
    n_top = min(MOBA_TOPK, nb)
    scale = HEAD_DIM ** -0.5
    gate = jnp.einsum("bqhd,bjhd->bqhj", q, k_means).astype(jnp.float32)
    cur = q_pos // MOBA_BLOCK
    fully_past = jnp.arange(nb)[None, :] < cur[:, None]
    gate = jnp.where(fully_past[None, :, None, :], gate, NEG)
    _, sel = lax.top_k(gate, n_top)
    sel_valid = sel < cur[None, :, None, None]
    k_sel, v_sel = fetch_sel(sel)
    l_sel = jnp.einsum("bqhd,bqhjkd->bqhjk", q, k_sel).astype(jnp.float32) * scale
    l_sel = jnp.where(sel_valid[..., None], l_sel, NEG).reshape(bq, nq, nh, n_top * MOBA_BLOCK)
    l_own = jnp.einsum("bqhd,bqkhd->bqhk", q, own_k).astype(jnp.float32) * scale
    l_own = jnp.where((own_pos <= q_pos[:, None])[None, :, None, :], l_own, NEG)
    p = jax.nn.softmax(jnp.concatenate([l_sel, l_own], axis=-1), axis=-1).astype(v_sel.dtype)
    p_sel = p[..., : n_top * MOBA_BLOCK].reshape(bq, nq, nh, n_top, MOBA_BLOCK)
    p_own = p[..., n_top * MOBA_BLOCK:]
    return (jnp.einsum("bqhjk,bqhjkd->bqhd", p_sel, v_sel)
            + jnp.einsum("bqhk,bqkhd->bqhd", p_own, own_v))


def gather_rows(t, idx):
    return jax.vmap(lambda tb, ib: tb[ib])(t, idx)


def gather_blocks(blk, sel):
    h = jnp.arange(blk.shape[3])[None, :, None]
    return jax.vmap(lambda bb, sb: jnp.transpose(bb, (2, 0, 1, 3))[h, sb])(blk, sel)


def fetch_rows(pool, new, page_table, pos, head=None):
    past = page_table.shape[1] * PAGE_SIZE
    in_past = pos < past
    pp = jnp.clip(pos, 0, past - 1)
    phys = jax.vmap(lambda pt, p: pt[p // PAGE_SIZE])(page_table, pp)
    row = pp % PAGE_SIZE
    nn = jnp.clip(pos - past, 0, new.shape[1] - 1)
    if head is None:
        from_pool = pool[phys, row]
        from_new = jax.vmap(lambda nb, i: nb[i])(new, nn)
    else:
        from_pool = pool[phys, row, head]
        from_new = jax.vmap(lambda nb, i: nb[i, head])(new, nn)
    mask = in_past.reshape(in_past.shape + (1,) * (from_pool.ndim - in_past.ndim))
    return jnp.where(mask, from_pool, from_new)


def prompt_mixers(qa, ka, va, qi, ki, wi, qb, kb, vb):
    bn, s = qa.shape[:2]
    nq = s // Q_BLOCK
    k_sel = min(TOPK_KEYS, s // 4)

    def dsa_block(i):
        q0 = i * Q_BLOCK
        sl = lambda t: lax.dynamic_slice_in_dim(t, q0, Q_BLOCK, axis=1)
        q_pos = q0 + jnp.arange(Q_BLOCK)
        fetch = lambda idx: (gather_rows(ka, idx), gather_rows(va, idx))
        return dsa_attend(sl(qa), sl(qi), sl(wi), q_pos, ki, fetch, k_sel)

    ya = lax.map(dsa_block, jnp.arange(nq))
    ya = jnp.moveaxis(ya, 0, 1).reshape(bn, s, WIDTH_A)

    kblk = to_blocks(kb)
    vblk = to_blocks(vb)
    k_means = block_means(kblk)
    blk_off = jnp.arange(MOBA_BLOCK)

    def moba_block(i):
        q0 = i * Q_BLOCK
        q_pos = q0 + jnp.arange(Q_BLOCK)
        own = q0 // MOBA_BLOCK
        own_k = lax.dynamic_index_in_dim(kblk, own, axis=1, keepdims=False)
        own_v = lax.dynamic_index_in_dim(vblk, own, axis=1, keepdims=False)
        shp = (bn, Q_BLOCK) + own_k.shape[1:]
        own_pos = jnp.broadcast_to((own * MOBA_BLOCK + blk_off)[None, :], (Q_BLOCK, MOBA_BLOCK))
        fetch_sel = lambda sel: (gather_blocks(kblk, sel), gather_blocks(vblk, sel))
        q = lax.dynamic_slice_in_dim(qb, q0, Q_BLOCK, axis=1)
        return moba_attend(q, q_pos, k_means, fetch_sel,
                           jnp.broadcast_to(own_k[:, None], shp),
                           jnp.broadcast_to(own_v[:, None], shp), own_pos)

    yb = lax.map(moba_block, jnp.arange(nq))
    yb = jnp.moveaxis(yb, 0, 1).reshape(bn, s, WIDTH_B)
    return ya, yb


def sample_mixers(qa, ka, va, qi, ki, wi, qb, kb, vb,
                  cache_ka, cache_va, cache_idx_k, cache_kb, cache_vb, page_table):
    bn, nq = qa.shape[:2]
    past = page_table.shape[1] * PAGE_SIZE
    L = past + nq
    q_pos = past + jnp.arange(nq)

    ki_all = jnp.concatenate([cache_idx_k[page_table].reshape(bn, past, IDX_DIM), ki], axis=1)
    fetch = lambda idx: (fetch_rows(cache_ka, ka, page_table, idx),
                         fetch_rows(cache_va, va, page_table, idx))
    ya = dsa_attend(qa, qi, wi, q_pos, ki_all, fetch, min(TOPK_KEYS, L // 4)).reshape(bn, nq, WIDTH_A)

    kb_all = jnp.concatenate([cache_kb[page_table].reshape(bn, past, N_HEADS_B, HEAD_DIM), kb], axis=1)
    k_means = block_means(to_blocks(kb_all))
    head = jnp.arange(N_HEADS_B)[:, None, None]
    blk_off = jnp.arange(MOBA_BLOCK)

    def fetch_sel(sel):
        pos = sel[..., None] * MOBA_BLOCK + blk_off
        return (fetch_rows(cache_kb, kb, page_table, pos, head),
                fetch_rows(cache_vb, vb, page_table, pos, head))

    own_pos = ((q_pos // MOBA_BLOCK) * MOBA_BLOCK)[:, None] + blk_off[None, :]
    own_idx = jnp.broadcast_to(own_pos[None], (bn, nq, MOBA_BLOCK))
    own_k = fetch_rows(cache_kb, kb, page_table, own_idx)
    own_v = fetch_rows(cache_vb, vb, page_table, own_idx)
    yb = moba_attend(qb, q_pos, k_means, fetch_sel, own_k, own_v, own_pos).reshape(bn, nq, WIDTH_B)
    return ya, yb


def merge_branches(ya, yb, ga, gb, w_proj_a, w_proj_b, w_out):
    za = jnp.einsum("bsw,wd->bsd", ya, w_proj_a)
    zb = jnp.einsum("bsw,wd->bsd", yb, w_proj_b)
    m = jax.nn.sigmoid(ga) * za + jax.nn.sigmoid(gb) * zb
    return jnp.einsum("bsd,de->bse", m, w_out)


def swiglu(xn, w_gate, w_up, w_down):
    h = jax.nn.silu(jnp.einsum("bsd,df->bsf", xn, w_gate)) * jnp.einsum("bsd,df->bsf", xn, w_up)
    return jnp.einsum("bsf,fd->bsd", h, w_down)


def setup_inputs(seed: int = 0) -> dict:
    key = jax.random.key(seed)
    ks = jax.random.split(key, 24)
    f32 = jnp.float32
    nrm = lambda k, shape, scale: jax.random.normal(k, shape, f32) * scale
    n_pages = PAST_LEN // PAGE_SIZE
    n_used = DEC_BATCH * n_pages
    n_pool = n_used + max(n_used // 4, 1)
    kv_shape = (DEPTH, n_pool, PAGE_SIZE, N_HEADS_A, HEAD_DIM)
    kvb_shape = (DEPTH, n_pool, PAGE_SIZE, N_HEADS_B, HEAD_DIM)
    page_table = jax.random.permutation(ks[0], n_pool)[:n_used].reshape(DEC_BATCH, n_pages).astype(jnp.int32)
    return {
        "x_prompt": nrm(ks[1], (BATCH, SEQ, D_MODEL), 1.0),
        "x_sample": nrm(ks[2], (DEC_BATCH, DEC_SEQ, D_MODEL), 1.0),
        "cache_ka": nrm(ks[3], kv_shape, 1.0),
        "cache_va": nrm(ks[4], kv_shape, 1.0),
        "cache_idx_k": nrm(ks[5], (DEPTH, n_pool, PAGE_SIZE, IDX_DIM), 1.0),
        "cache_kb": nrm(ks[6], kvb_shape, 1.0),
        "cache_vb": nrm(ks[7], kvb_shape, 1.0),
        "page_table": page_table,
        "g_attn": 1.0 + nrm(ks[8], (DEPTH, D_MODEL), 0.05),
        "w_in": nrm(ks[9], (DEPTH, D_MODEL, D_IN), D_MODEL ** -0.5),
        "g_idx_k": 1.0 + nrm(ks[10], (DEPTH, IDX_DIM), 0.05),
        "w_proj_a": nrm(ks[11], (DEPTH, WIDTH_A, D_MODEL), WIDTH_A ** -0.5),
        "w_proj_b": nrm(ks[12], (DEPTH, WIDTH_B, D_MODEL), WIDTH_B ** -0.5),
        "w_out": nrm(ks[13], (DEPTH, D_MODEL, D_MODEL), D_MODEL ** -0.5),
        "g_ffn": 1.0 + nrm(ks[14], (DEPTH, D_MODEL), 0.05),
        "w_gate": nrm(ks[15], (DEPTH, D_MODEL, D_FF), D_MODEL ** -0.5),
        "w_up": nrm(ks[16], (DEPTH, D_MODEL, D_FF), D_MODEL ** -0.5),
        "w_down": nrm(ks[17], (DEPTH, D_FF, D_MODEL), D_FF ** -0.5),
        "g_final": 1.0 + nrm(ks[18], (D_MODEL,), 0.05),
    }


def reference(x_prompt, x_sample, cache_ka, cache_va, cache_idx_k, cache_kb, cache_vb, page_table,
              g_attn, w_in, g_idx_k, w_proj_a, w_proj_b, w_out, g_ffn, w_gate, w_up, w_down, g_final):
    pos_p = jnp.arange(x_prompt.shape[1])
    pos_s = page_table.shape[1] * PAGE_SIZE + jnp.arange(x_sample.shape[1])
    x_p, x_s = x_prompt, x_sample
    new_p = [[] for _ in range(5)]
    new_s = [[] for _ in range(5)]
    for l in range(DEPTH):
        qa, ka, va, qi, ki, wi, qb, kb, vb, ga, gb = mixer_inputs(rms_norm(x_p, g_attn[l]), w_in[l], g_idx_k[l], pos_p)
        ya, yb = prompt_mixers(qa, ka, va, qi, ki, wi, qb, kb, vb)
        x_p = x_p + merge_branches(ya, yb, ga, gb, w_proj_a[l], w_proj_b[l], w_out[l])
        x_p = x_p + swiglu(rms_norm(x_p, g_ffn[l]), w_gate[l], w_up[l], w_down[l])
        for lst, t in zip(new_p, (ka, va, ki, kb, vb)):
            lst.append(t)
        qa, ka, va, qi, ki, wi, qb, kb, vb, ga, gb = mixer_inputs(rms_norm(x_s, g_attn[l]), w_in[l], g_idx_k[l], pos_s)
        ya, yb = sample_mixers(qa, ka, va, qi, ki, wi, qb, kb, vb,
                               cache_ka[l], cache_va[l], cache_idx_k[l], cache_kb[l], cache_vb[l], page_table)
        x_s = x_s + merge_branches(ya, yb, ga, gb, w_proj_a[l], w_proj_b[l], w_out[l])
        x_s = x_s + swiglu(rms_norm(x_s, g_ffn[l]), w_gate[l], w_up[l], w_down[l])
        for lst, t in zip(new_s, (ka, va, ki, kb, vb)):
            lst.append(t)
    y_prompt = rms_norm(x_p, g_final)
    y_sample = rms_norm(x_s, g_final)
    new_ka_p = jnp.stack(new_p[0])
    new_va_p = jnp.stack(new_p[1])
    new_idx_k_p = jnp.stack(new_p[2])
    new_kb_p = jnp.stack(new_p[3])
    new_vb_p = jnp.stack(new_p[4])
    new_ka_s = jnp.stack(new_s[0])
    new_va_s = jnp.stack(new_s[1])
    new_idx_k_s = jnp.stack(new_s[2])
    new_kb_s = jnp.stack(new_s[3])
    new_vb_s = jnp.stack(new_s[4])
    return (y_prompt, y_sample, new_ka_p, new_va_p, new_idx_k_p, new_kb_p, new_vb_p,
            new_ka_s, new_va_s, new_idx_k_s, new_kb_s, new_vb_s)
```

```python
import functools

import jax
import jax.numpy as jnp
from jax import lax
from jax.experimental import pallas as pl
from jax.experimental.pallas import tpu as pltpu

F32 = jnp.float32
BF16 = jnp.bfloat16
I32 = jnp.int32

HEAD_DIM = 64
N_HEADS = 8
WIDTH = N_HEADS * HEAD_DIM
IDX_DIM = 64
N_IDX_HEADS = 8
TOPK_KEYS = 256
MOBA_BLOCK = 256
MOBA_TOPK = 3
PAGE_SIZE = 128
Q_BLOCK = 128
ROPE_THETA = 10000.0
RMS_EPS = 1e-6
NEG = -1e30
LANES = 128
INT_MIN = -(2 ** 31)
VMEM_LIMIT = 56 * 1024 * 1024

_NT = (((1,), (1,)), ((), ()))


def _dot_nt(a, b):
    return lax.dot_general(a, b, _NT, preferred_element_type=F32)


def _dot(a, b):
    return jnp.dot(a, b, preferred_element_type=F32)


def _sortable(score):
    bits = lax.bitcast_convert_type(score, I32)
    key = bits ^ ((bits >> 31) & jnp.int32(0x7FFFFFFF))
    return jnp.where(score == 0.0, jnp.int32(0), key)


def _params(*sem):
    return pltpu.CompilerParams(dimension_semantics=sem, vmem_limit_bytes=VMEM_LIMIT)


def _const_spec(shape):
    nd = len(shape)
    return pl.BlockSpec(shape, lambda *_: (0,) * nd, pipeline_mode=pl.Buffered(1))


def _rope_tables(pos):
    half = HEAD_DIM // 2
    inv_freq = ROPE_THETA ** (-jnp.arange(0, HEAD_DIM, 2, dtype=F32) / HEAD_DIM)
    ang = pos.astype(F32)[:, None] * inv_freq[None, :]
    cos, sin = jnp.cos(ang), jnp.sin(ang)
    cs = jnp.concatenate([cos, cos, cos, cos], axis=-1)
    sn = jnp.concatenate([-sin, sin, -sin, sin], axis=-1)
    del half
    return cs, sn


def _proj_kernel(x_ref, g_ref, w_ref, cs_ref, sn_ref, gk_ref,
                 qa_ref, qi_ref, qb_ref, kaf_ref, vaf_ref, kbf_ref, vbf_ref,
                 kab_ref, vab_ref, kbb_ref, vbb_ref, ga_ref, gb_ref, kwf_ref, kwb_ref, kmean_ref):
    tm = x_ref.shape[0]
    x = x_ref[...]
    ms = jnp.mean(x * x, axis=-1, keepdims=True)
    xn = (x * lax.rsqrt(ms + RMS_EPS) * g_ref[...]).astype(BF16)
    cs = cs_ref[...]
    sn = sn_ref[...]
    lane = lax.broadcasted_iota(I32, (tm, LANES), 1)
    first_half = (lane & (HEAD_DIM // 2)) == 0

    def rope(p):
        swapped = jnp.where(first_half, pltpu.roll(p, LANES - HEAD_DIM // 2, 1),
                            pltpu.roll(p, HEAD_DIM // 2, 1))
        return p * cs + swapped * sn

    def slab(j, width=WIDTH):
        return _dot(xn, w_ref[:, j * WIDTH:j * WIDTH + width])

    def rope_slab(h):
        return jnp.concatenate(
            [rope(h[:, k * LANES:(k + 1) * LANES]) for k in range(WIDTH // LANES)], axis=1)

    qa_ref[...] = rope_slab(slab(0)).astype(BF16)
    qi_ref[...] = rope_slab(slab(1)).astype(BF16)
    qb_ref[...] = rope_slab(slab(2)).astype(BF16)
    ka = rope_slab(slab(3))
    kaf_ref[...] = ka
    kab_ref[...] = ka.astype(BF16)
    va = slab(4)
    vaf_ref[...] = va
    vab_ref[...] = va.astype(BF16)
    kb = rope_slab(slab(5))
    kbf_ref[...] = kb
    kbb_ref[...] = kb.astype(BF16)
    for r in range(tm // MOBA_BLOCK):
        kmean_ref[0, r:r + 1, :] = jnp.mean(kb[r * MOBA_BLOCK:(r + 1) * MOBA_BLOCK], axis=0, keepdims=True)
    vb = slab(6)
    vbf_ref[...] = vb
    vbb_ref[...] = vb.astype(BF16)
    ga_ref[:, :WIDTH] = slab(7)
    ga_ref[:, WIDTH:] = slab(8)
    gb_ref[:, :WIDTH] = slab(9)
    gb_ref[:, WIDTH:] = slab(10)
    t = slab(11, LANES)
    is_key = lane < IDX_DIM
    kms = jnp.sum(jnp.where(is_key, t * t, 0.0), axis=-1, keepdims=True) * (1.0 / IDX_DIM)
    kn = rope(t * lax.rsqrt(kms + RMS_EPS) * gk_ref[...])
    kw = jnp.where(is_key, kn, jnp.where(lane < IDX_DIM + N_IDX_HEADS, t * (N_IDX_HEADS ** -0.5), 0.0))
    kwf_ref[...] = kw
    kwb_ref[...] = kw.astype(BF16)


def _arrange_w_in(w_in):
    d = w_in.shape[0]
    o = [0]
    for s in (WIDTH, WIDTH, WIDTH, N_IDX_HEADS * IDX_DIM, IDX_DIM, N_IDX_HEADS, WIDTH, WIDTH, WIDTH, d, d):
        o.append(o[-1] + s)
    qa, ka, va, qi, ki, wi, qb, kb, vb, ga, gb = [w_in[:, o[i]:o[i + 1]] for i in range(11)]
    qscale = HEAD_DIM ** -0.5
    iscale = IDX_DIM ** -0.5
    pad = jnp.zeros((d, LANES - IDX_DIM - N_IDX_HEADS), w_in.dtype)
    return jnp.concatenate([qa * qscale, qi * iscale, qb * qscale, ka, va, kb, vb, ga, gb, ki, wi, pad],
                           axis=1).astype(BF16)


def _proj_call(x, g_attn, w_arr, cs, sn, gk, tm):
    m, d = x.shape
    nt = m // tm
    npos = cs.shape[0] // tm
    row = lambda w: pl.BlockSpec((tm, w), lambda i: (i, 0))
    tab = pl.BlockSpec((tm, LANES), lambda i: (i % npos, 0))
    nblk = max(tm // MOBA_BLOCK, 1)
    out_shape = (
        [jax.ShapeDtypeStruct((m, WIDTH), BF16)] * 3
        + [jax.ShapeDtypeStruct((m, WIDTH), F32)] * 4
        + [jax.ShapeDtypeStruct((m, WIDTH), BF16)] * 4
        + [jax.ShapeDtypeStruct((m, d), F32)] * 2
        + [jax.ShapeDtypeStruct((m, LANES), F32), jax.ShapeDtypeStruct((m, LANES), BF16),
           jax.ShapeDtypeStruct((nt, nblk, WIDTH), F32)])
    out_specs = ([row(WIDTH)] * 11 + [row(d)] * 2 + [row(LANES)] * 2
                 + [pl.BlockSpec((1, nblk, WIDTH), lambda i: (i, 0, 0))])
    return pl.pallas_call(
        _proj_kernel,
        grid=(nt,),
        in_specs=[row(d), _const_spec((1, d)), _const_spec(w_arr.shape), tab, tab, _const_spec((1, LANES))],
        out_specs=out_specs,
        out_shape=out_shape,
        compiler_params=_params("parallel"),
        name="proj",
    )(x, g_attn.reshape(1, d), w_arr, cs, sn, gk)


def _count_rows(key_ref, nchunks, pred):
    _, rows, width = key_ref.shape

    def body(c, cnt):
        k0 = c * width
        kk = key_ref[c]
        for j in range(width // LANES):
            cnt = cnt + jnp.where(pred(kk[:, j * LANES:(j + 1) * LANES], k0 + j * LANES), 1.0, 0.0)
        return cnt

    cnt = lax.fori_loop(0, nchunks, body, jnp.zeros((rows, LANES), F32))
    return jnp.sum(cnt, axis=-1, keepdims=True)


def _select_threshold(key_ref, nchunks, k_sel, idx_bits, y_ref):
    rows = key_ref.shape[1]
    rep = lambda v: jnp.broadcast_to(v, (rows, LANES))
    kf = float(k_sel)

    def bis(t, thr):
        cand = thr + lax.shift_left(jnp.int32(1), 31 - t)
        tot = _count_rows(key_ref, nchunks, lambda kk, _: kk >= cand)
        return jnp.where(rep(tot) >= kf, cand, thr)

    thr = lax.fori_loop(0, 32, bis, jnp.full((rows, LANES), INT_MIN, I32))
    c_ge = _count_rows(key_ref, nchunks, lambda kk, _: kk >= thr)
    y_ref[...] = jnp.full((rows, LANES), 2 ** 30, I32)

    @pl.when(jnp.max(c_ge) > kf)
    def _():
        c_gt = _count_rows(key_ref, nchunks, lambda kk, _: kk > thr)
        need = rep(kf - c_gt)
        lane = lax.broadcasted_iota(I32, (rows, LANES), 1)

        def ybis(t, y):
            cand = y + lax.shift_left(jnp.int32(1), idx_bits - 1 - t)
            f = _count_rows(key_ref, nchunks, lambda kk, c0: (kk == thr) & ((c0 + lane) < cand))
            return jnp.where(rep(f) < need, cand, y)

        y_ref[...] = lax.fori_loop(0, idx_bits, ybis, jnp.zeros((rows, LANES), I32))

    return thr


def _online_update(h, q, k, v, mask, m_ref, l_ref, acc_ref):
    hs = slice(h * HEAD_DIM, (h + 1) * HEAD_DIM)
    lg = jnp.where(mask, _dot_nt(q, k), NEG)
    m_old = m_ref[h]
    m_new = jnp.maximum(m_old, jnp.max(lg, axis=-1, keepdims=True))
    alpha = jnp.exp(m_old - m_new)
    p = jnp.where(mask, jnp.exp(lg - m_new[:, :1]), 0.0)
    l_ref[h] = alpha * l_ref[h] + jnp.sum(p, axis=-1, keepdims=True)
    m_ref[h] = m_new
    acc_ref[:, hs] = alpha[:, :HEAD_DIM] * acc_ref[:, hs] + _dot(p.astype(BF16), v)


def _finish_heads(o_ref, l_ref, acc_ref):
    for h in range(N_HEADS):
        hs = slice(h * HEAD_DIM, (h + 1) * HEAD_DIM)
        o_ref[:, hs] = (acc_ref[:, hs] / l_ref[h][:, :HEAD_DIM]).astype(o_ref.dtype)


DSA_CHUNK = 512


def _dsa_prompt_kernel(qa_ref, qi_ref, kwq_ref, kwk_ref, ka_ref, va_ref, o_ref,
                       key_ref, wib_ref, y_ref, m_ref, l_ref, acc_ref, *, k_sel, idx_bits):
    i = pl.program_id(1)
    c_w = DSA_CHUNK
    nkc = (i * Q_BLOCK) // c_w + 1
    row = lax.broadcasted_iota(I32, (Q_BLOCK, c_w), 0)
    col = lax.broadcasted_iota(I32, (Q_BLOCK, c_w), 1)
    qpos = i * Q_BLOCK + row
    wi = kwq_ref[...]
    for h in range(N_IDX_HEADS):
        wib_ref[h] = jnp.broadcast_to(wi[:, IDX_DIM + h:IDX_DIM + h + 1], (Q_BLOCK, LANES))

    def score_body(c, carry):
        k0 = pl.multiple_of(c * c_w, c_w)
        kic = kwk_ref[pl.ds(k0, c_w), :][:, :IDX_DIM]
        acc = jnp.zeros((Q_BLOCK, c_w), F32)
        for h in range(N_IDX_HEADS):
            s = _dot_nt(qi_ref[:, h * IDX_DIM:(h + 1) * IDX_DIM], kic)
            w = wib_ref[h]
            acc = acc + jnp.maximum(s, 0.0) * jnp.concatenate([w] * (c_w // LANES), axis=1)
        sc = jnp.where((k0 + col) <= qpos, acc, NEG)
        key_ref[c] = _sortable(sc)
        return carry

    lax.fori_loop(0, nkc, score_body, 0)
    thr = _select_threshold(key_ref, nkc, k_sel, idx_bits, y_ref)
    thr_w = jnp.concatenate([thr] * (c_w // LANES), axis=1)
    y_w = jnp.concatenate([y_ref[...]] * (c_w // LANES), axis=1)

    m_ref[...] = jnp.full(m_ref.shape, NEG, F32)
    l_ref[...] = jnp.zeros(l_ref.shape, F32)
    acc_ref[...] = jnp.zeros(acc_ref.shape, F32)

    def att_body(c, carry):
        k0 = pl.multiple_of(c * c_w, c_w)
        kk = key_ref[c]
        kpos = k0 + col
        sel = ((kk > thr_w) | ((kk == thr_w) & (kpos <= y_w))) & (kpos <= qpos)
        kc = ka_ref[pl.ds(k0, c_w), :]
        vc = va_ref[pl.ds(k0, c_w), :]
        for h in range(N_HEADS):
            hs = slice(h * HEAD_DIM, (h + 1) * HEAD_DIM)
            _online_update(h, qa_ref[:, hs], kc[:, hs], vc[:, hs], sel, m_ref, l_ref, acc_ref)
        return carry

    lax.fori_loop(0, nkc, att_body, 0)
    _finish_heads(o_ref, l_ref, acc_ref)


def _dsa_prompt_call(qa, qi, kwf, kwb, ka, va, bn, s):
    nq = s // Q_BLOCK
    k_sel = min(TOPK_KEYS, s // 4)
    idx_bits = max(int(s).bit_length(), 1)
    qspec = lambda w: pl.BlockSpec((Q_BLOCK, w), lambda b, i: (b * nq + i, 0))
    seq = lambda w: pl.BlockSpec((s, w), lambda b, i: (b, 0))
    kern = functools.partial(_dsa_prompt_kernel, k_sel=k_sel, idx_bits=idx_bits)
    return pl.pallas_call(
        kern,
        grid=(bn, nq),
        in_specs=[qspec(WIDTH), qspec(WIDTH), qspec(LANES), seq(LANES), seq(WIDTH), seq(WIDTH)],
        out_specs=qspec(WIDTH),
        out_shape=jax.ShapeDtypeStruct((bn * s, WIDTH), BF16),
        scratch_shapes=[
            pltpu.VMEM((s // DSA_CHUNK, Q_BLOCK, DSA_CHUNK), I32),
            pltpu.VMEM((N_IDX_HEADS, Q_BLOCK, LANES), F32),
            pltpu.VMEM((Q_BLOCK, LANES), I32),
            pltpu.VMEM((N_HEADS, Q_BLOCK, LANES), F32),
            pltpu.VMEM((N_HEADS, Q_BLOCK, LANES), F32),
            pltpu.VMEM((Q_BLOCK, WIDTH), F32),
        ],
        compiler_params=_params("parallel", "arbitrary"),
        name="dsa_prompt",
    )(qa, qi, kwf, kwb, ka, va)


def _top_blocks(gate, n_valid_lane_mask, n_top):
    rows = gate.shape[0]
    lanef = lax.broadcasted_iota(I32, (rows, LANES), 1).astype(F32)
    g = jnp.where(n_valid_lane_mask, gate, NEG)
    marks = jnp.zeros((rows, LANES), F32)
    for _ in range(n_top):
        mx = jnp.max(g, axis=-1, keepdims=True)
        first = jnp.min(jnp.where(g == mx, lanef, float(LANES)), axis=-1, keepdims=True)
        hit = lanef == first
        marks = jnp.where(hit, 1.0, marks)
        g = jnp.where(hit, -jnp.inf, g)
    return jnp.where(n_valid_lane_mask, marks, 0.0)


def _moba_prompt_kernel(qb_ref, kmean_ref, kb_ref, vb_ref, o_ref, selw_ref, m_ref, l_ref, acc_ref):
    i = pl.program_id(1)
    cur = (i * Q_BLOCK) // MOBA_BLOCK
    lane = lax.broadcasted_iota(I32, (Q_BLOCK, LANES), 1)
    km = kmean_ref[0].astype(BF16)
    past = lane < cur
    for h in range(N_HEADS):
        hs = slice(h * HEAD_DIM, (h + 1) * HEAD_DIM)
        selw_ref[h] = _top_blocks(_dot_nt(qb_ref[:, hs], km[:, hs]), past, MOBA_TOPK)

    m_ref[...] = jnp.full(m_ref.shape, NEG, F32)
    l_ref[...] = jnp.zeros(l_ref.shape, F32)
    acc_ref[...] = jnp.zeros(acc_ref.shape, F32)

    def blk_body(j, carry):
        k0 = pl.multiple_of(j * MOBA_BLOCK, MOBA_BLOCK)
        kc = kb_ref[pl.ds(k0, MOBA_BLOCK), :]
        vc = vb_ref[pl.ds(k0, MOBA_BLOCK), :]
        for h in range(N_HEADS):
            hs = slice(h * HEAD_DIM, (h + 1) * HEAD_DIM)
            picked = jnp.sum(jnp.where(lane == j, selw_ref[h], 0.0), axis=-1, keepdims=True) > 0.5
            mask = jnp.broadcast_to(picked, (Q_BLOCK, MOBA_BLOCK))
            _online_update(h, qb_ref[:, hs], kc[:, hs], vc[:, hs], mask, m_ref, l_ref, acc_ref)
        return carry

    lax.fori_loop(0, cur, blk_body, 0)

    k0 = pl.multiple_of(cur * MOBA_BLOCK, MOBA_BLOCK)
    kc = kb_ref[pl.ds(k0, MOBA_BLOCK), :]
    vc = vb_ref[pl.ds(k0, MOBA_BLOCK), :]
    row = lax.broadcasted_iota(I32, (Q_BLOCK, MOBA_BLOCK), 0)
    col = lax.broadcasted_iota(I32, (Q_BLOCK, MOBA_BLOCK), 1)
    causal = (k0 + col) <= (i * Q_BLOCK + row)
    for h in range(N_HEADS):
        hs = slice(h * HEAD_DIM, (h + 1) * HEAD_DIM)
        _online_update(h, qb_ref[:, hs], kc[:, hs], vc[:, hs], causal, m_ref, l_ref, acc_ref)
    _finish_heads(o_ref, l_ref, acc_ref)


def _moba_prompt_call(qb, kmean, kb, vb, bn, s):
    nq = s // Q_BLOCK
    qspec = pl.BlockSpec((Q_BLOCK, WIDTH), lambda b, i: (b * nq + i, 0))
    seq = pl.BlockSpec((s, WIDTH), lambda b, i: (b, 0))
    return pl.pallas_call(
        _moba_prompt_kernel,
        grid=(bn, nq),
        in_specs=[qspec, pl.BlockSpec((1, LANES, WIDTH), lambda b, i: (b, 0, 0)), seq, seq],
        out_specs=qspec,
        out_shape=jax.ShapeDtypeStruct((bn * s, WIDTH), BF16),
        scratch_shapes=[
            pltpu.VMEM((N_HEADS, Q_BLOCK, LANES), F32),
            pltpu.VMEM((N_HEADS, Q_BLOCK, LANES), F32),
            pltpu.VMEM((N_HEADS, Q_BLOCK, LANES), F32),
            pltpu.VMEM((Q_BLOCK, WIDTH), F32),
        ],
        compiler_params=_params("parallel", "arbitrary"),
        name="moba_prompt",
    )(qb, kmean, kb, vb)


def _sample_a_kernel(pt_ref, qi_ref, wcol_ref, qda_ref, qdb_ref, idx_ref, ka_ref, kb_ref,
                     kin_ref, kan_ref, kbn_ref, pa_ref, pb_ref,
                     key_ref, y_ref, la_ref, lb_ref, ksum_ref, *, n_pages, nq, k_sel, idx_bits):
    del pt_ref
    p = pl.program_id(1)
    past = n_pages * PAGE_SIZE
    nch = n_pages + 1
    rows = N_HEADS * nq
    lane = lax.broadcasted_iota(I32, (rows, LANES), 1)
    qrow = lax.broadcasted_iota(I32, (rows, LANES), 0) % nq

    def idx_scores(kpage):
        s = jnp.maximum(_dot_nt(qi_ref[0], kpage), 0.0) * wcol_ref[0]
        sc = s[0:nq]
        for h in range(1, N_IDX_HEADS):
            sc = sc + s[h * nq:(h + 1) * nq]
        return sc

    @pl.when(p == 0)
    def _():
        ksum_ref[...] = jnp.zeros(ksum_ref.shape, F32)

    key_ref[p] = _sortable(idx_scores(idx_ref[0].astype(BF16)))
    la_ref[p] = _dot_nt(qda_ref[0], ka_ref[0].astype(BF16))
    kbp = kb_ref[0]
    lb_ref[p] = _dot_nt(qdb_ref[0], kbp.astype(BF16))
    blk = (p * PAGE_SIZE) // MOBA_BLOCK
    ksum_ref[pl.ds(blk, 1), :] = ksum_ref[pl.ds(blk, 1), :] + jnp.sum(kbp, axis=0, keepdims=True)

    def softmax_chunks(l_ref, p_ref, mx):
        m = jnp.max(mx, axis=-1, keepdims=True)

        def exp_body(c, ssum):
            e = jnp.exp(l_ref[c] - m)
            l_ref[c] = e
            return ssum + e

        ssum = lax.fori_loop(0, nch, exp_body, jnp.zeros((rows, LANES), F32))
        inv = 1.0 / jnp.sum(ssum, axis=-1, keepdims=True)

        def out_body(c, carry):
            p_ref[0, c] = (l_ref[c] * inv).astype(p_ref.dtype)
            return carry

        lax.fori_loop(0, nch, out_body, 0)

    @pl.when(p == n_pages - 1)
    def _():
        new_ok = (lane <= qrow) & (lane < nq)
        neg = jnp.full((rows, LANES), NEG, F32)
        key_ref[n_pages] = _sortable(jnp.where(new_ok[:nq], idx_scores(kin_ref[0]), NEG))
        thr = _select_threshold(key_ref, nch, k_sel, idx_bits, y_ref)
        yv = y_ref[...]
        la_ref[n_pages] = jnp.where(new_ok, _dot_nt(qda_ref[0], kan_ref[0]), NEG)

        def mask_a(c, mx):
            kk = key_ref[c]
            sel = (kk > thr) | ((kk == thr) & ((c * LANES + lane[:nq]) <= yv))
            lg = jnp.where(jnp.concatenate([sel] * N_HEADS, axis=0), la_ref[c], NEG)
            la_ref[c] = lg
            return jnp.maximum(mx, lg)

        softmax_chunks(la_ref, pa_ref, lax.fori_loop(0, nch, mask_a, neg))
        km = (ksum_ref[...] * (1.0 / MOBA_BLOCK)).astype(BF16)
        marks = _top_blocks(_dot_nt(qdb_ref[0], km), lane < past // MOBA_BLOCK, MOBA_TOPK)

        def mask_b(c, mx):
            blk_c = (c * PAGE_SIZE) // MOBA_BLOCK
            picked = jnp.sum(jnp.where(lane == blk_c, marks, 0.0), axis=-1, keepdims=True) > 0.5
            lg = jnp.where(picked, lb_ref[c], NEG)
            lb_ref[c] = lg
            return jnp.maximum(mx, lg)

        mx_b = lax.fori_loop(0, n_pages, mask_b, neg)
        lg_new = jnp.where(new_ok, _dot_nt(qdb_ref[0], kbn_ref[0]), NEG)
        lb_ref[n_pages] = lg_new
        softmax_chunks(lb_ref, pb_ref, jnp.maximum(mx_b, lg_new))


def _sample_a_call(page_table, qi_rows, wcol, qda, qdb, c_idx, c_ka, c_kb, ki_new, ka_new, kb_new, nq):
    bn, n_pages = page_table.shape
    rows = N_HEADS * nq
    nch = n_pages + 1
    k_sel = min(TOPK_KEYS, (n_pages * PAGE_SIZE + nq) // 4)
    idx_bits = int(nch * LANES).bit_length()
    per_b = lambda shp: pl.BlockSpec((1,) + shp, lambda b, p, pt: (b,) + (0,) * len(shp))
    page = lambda w: pl.BlockSpec((1, PAGE_SIZE, w), lambda b, p, pt: (pt[b, p], 0, 0))
    kern = functools.partial(_sample_a_kernel, n_pages=n_pages, nq=nq, k_sel=k_sel, idx_bits=idx_bits)
    grid_spec = pltpu.PrefetchScalarGridSpec(
        num_scalar_prefetch=1,
        grid=(bn, n_pages),
        in_specs=[per_b((rows, IDX_DIM)), per_b((rows, LANES)), per_b((rows, WIDTH)), per_b((rows, WIDTH)),
                  page(IDX_DIM), page(WIDTH), page(WIDTH),
                  per_b((LANES, IDX_DIM)), per_b((LANES, WIDTH)), per_b((LANES, WIDTH))],
        out_specs=[per_b((nch, rows, LANES)), per_b((nch, rows, LANES))],
        scratch_shapes=[
            pltpu.VMEM((nch, nq, LANES), I32),
            pltpu.VMEM((nq, LANES), I32),
            pltpu.VMEM((nch, rows, LANES), F32),
            pltpu.VMEM((nch, rows, LANES), F32),
            pltpu.VMEM((LANES, WIDTH), F32),
        ],
    )
    return pl.pallas_call(
        kern,
        grid_spec=grid_spec,
        out_shape=[jax.ShapeDtypeStruct((bn, nch, rows, LANES), BF16)] * 2,
        compiler_params=_params("parallel", "arbitrary"),
        name="sample_a",
    )(page_table, qi_rows, wcol, qda, qdb, c_idx, c_ka, c_kb, ki_new, ka_new, kb_new)


def _sample_b_kernel(pt_ref, pa_ref, pb_ref, pan_ref, pbn_ref, va_ref, vb_ref, van_ref, vbn_ref,
                     oa_ref, ob_ref, acca_ref, accb_ref, *, n_pages, nq):
    del pt_ref
    p = pl.program_id(1)

    @pl.when(p == 0)
    def _():
        acca_ref[...] = _dot(pan_ref[0, 0], van_ref[0])
        accb_ref[...] = _dot(pbn_ref[0, 0], vbn_ref[0])

    acca_ref[...] += _dot(pa_ref[0, 0], va_ref[0].astype(BF16))
    accb_ref[...] += _dot(pb_ref[0, 0], vb_ref[0].astype(BF16))

    @pl.when(p == n_pages - 1)
    def _():
        head_of_lane = lax.broadcasted_iota(I32, (nq, WIDTH), 1) // HEAD_DIM
        for acc_ref, o_ref in ((acca_ref, oa_ref), (accb_ref, ob_ref)):
            out = jnp.zeros((nq, WIDTH), F32)
            for h in range(N_HEADS):
                out = out + jnp.where(head_of_lane == h, acc_ref[h * nq:(h + 1) * nq, :], 0.0)
            o_ref[0] = out.astype(o_ref.dtype)


def _sample_b_call(page_table, pa, pb, c_va, c_vb, va_new, vb_new, nq):
    bn, n_pages = page_table.shape
    rows = N_HEADS * nq
    per_b = lambda shp: pl.BlockSpec((1,) + shp, lambda b, p, pt: (b, 0, 0))
    page = lambda w: pl.BlockSpec((1, PAGE_SIZE, w), lambda b, p, pt: (pt[b, p], 0, 0))
    pcol = pl.BlockSpec((1, 1, rows, LANES), lambda b, p, pt: (b, p, 0, 0))
    pnew = pl.BlockSpec((1, 1, rows, LANES), lambda b, p, pt: (b, n_pages, 0, 0))
    kern = functools.partial(_sample_b_kernel, n_pages=n_pages, nq=nq)
    grid_spec = pltpu.PrefetchScalarGridSpec(
        num_scalar_prefetch=1,
        grid=(bn, n_pages),
        in_specs=[pcol, pcol, pnew, pnew, page(WIDTH), page(WIDTH), per_b((LANES, WIDTH)), per_b((LANES, WIDTH))],
        out_specs=[per_b((nq, WIDTH)), per_b((nq, WIDTH))],
        scratch_shapes=[pltpu.VMEM((rows, WIDTH), F32), pltpu.VMEM((rows, WIDTH), F32)],
    )
    return pl.pallas_call(
        kern,
        grid_spec=grid_spec,
        out_shape=[jax.ShapeDtypeStruct((bn, nq, WIDTH), BF16)] * 2,
        compiler_params=_params("parallel", "arbitrary"),
        name="sample_b",
    )(page_table, pa, pb, pa, pb, c_va, c_vb, va_new, vb_new)


def _merge_ffn_kernel(x_ref, ya_ref, yb_ref, ga_ref, gb_ref, wa_ref, wb_ref, wo_ref, gf_ref,
                      wg_ref, wu_ref, wd_ref, gfin_ref, y_ref, *, ff_slab):
    za = _dot(ya_ref[...], wa_ref[...])
    zb = _dot(yb_ref[...], wb_ref[...])
    mix = jax.nn.sigmoid(ga_ref[...]) * za + jax.nn.sigmoid(gb_ref[...]) * zb
    x1 = x_ref[...] + _dot(mix.astype(BF16), wo_ref[...])
    ms = jnp.mean(x1 * x1, axis=-1, keepdims=True)
    xn = (x1 * lax.rsqrt(ms + RMS_EPS) * gf_ref[...]).astype(BF16)
    x2 = x1
    d_ff = wg_ref.shape[1]
    for s0 in range(0, d_ff, ff_slab):
        hg = _dot(xn, wg_ref[:, s0:s0 + ff_slab])
        hu = _dot(xn, wu_ref[:, s0:s0 + ff_slab])
        act = (hg * jax.nn.sigmoid(hg) * hu).astype(BF16)
        x2 = x2 + _dot(act, wd_ref[s0:s0 + ff_slab, :])
    ms2 = jnp.mean(x2 * x2, axis=-1, keepdims=True)
    y_ref[...] = x2 * lax.rsqrt(ms2 + RMS_EPS) * gfin_ref[...]


def _merge_ffn_call(x, ya, yb, ga, gb, wa, wb, wo, gf, wg, wu, wd, gfin, tm):
    m, d = x.shape
    d_ff = wg.shape[1]
    ff_slab = d_ff // 2 if (d_ff // 2) % LANES == 0 else d_ff
    row = lambda w: pl.BlockSpec((tm, w), lambda i: (i, 0))
    kern = functools.partial(_merge_ffn_kernel, ff_slab=ff_slab)
    return pl.pallas_call(
        kern,
        grid=(m // tm,),
        in_specs=[row(d), row(WIDTH), row(WIDTH), row(d), row(d),
                  _const_spec(wa.shape), _const_spec(wb.shape), _const_spec(wo.shape), _const_spec((1, d)),
                  _const_spec(wg.shape), _const_spec(wu.shape), _const_spec(wd.shape), _const_spec((1, d))],
        out_specs=row(d),
        out_shape=jax.ShapeDtypeStruct((m, d), F32),
        compiler_params=_params("parallel"),
        name="merge_ffn",
    )(x, ya, yb, ga, gb, wa, wb, wo, gf.reshape(1, d), wg, wu, wd, gfin.reshape(1, d))


def _pad_rows(t, rows):
    return jnp.pad(t, ((0, 0), (0, rows - t.shape[1]), (0, 0)))


def _layer(x_p, x_s, cache_ka, cache_va, cache_idx_k, cache_kb, cache_vb, page_table,
           g_attn, w_in, g_idx_k, w_proj_a, w_proj_b, w_out, g_ffn, w_gate, w_up, w_down, g_final):
    bn, s, d = x_p.shape
    db, nq, _ = x_s.shape
    n_pages = page_table.shape[1]
    n_pool = cache_ka.shape[0]
    past = n_pages * PAGE_SIZE

    w_arr = _arrange_w_in(w_in)
    gk = jnp.pad(g_idx_k, (0, LANES - IDX_DIM)).reshape(1, LANES)
    wa, wb, wo = w_proj_a.astype(BF16), w_proj_b.astype(BF16), w_out.astype(BF16)
    wg, wu, wd = w_gate.astype(BF16), w_up.astype(BF16), w_down.astype(BF16)

    cs_p, sn_p = _rope_tables(jnp.arange(s))
    tm_p = 512
    (qa, qi, qb, kaf, vaf, kbf, vbf, kab, vab, kbb, vbb, ga, gb, kwf, kwb, kmean) = _proj_call(
        x_p.reshape(bn * s, d), g_attn, w_arr, cs_p, sn_p, gk, tm_p)
    ya = _dsa_prompt_call(qa, qi, kwf, kwb, kab, vab, bn, s)
    kmean_b = _pad_rows(kmean.reshape(bn, s // MOBA_BLOCK, WIDTH), LANES)
    yb = _moba_prompt_call(qb, kmean_b, kbb, vbb, bn, s)
    y_p = _merge_ffn_call(x_p.reshape(bn * s, d), ya, yb, ga, gb, wa, wb, wo, g_ffn, wg, wu, wd, g_final, 256)
    new_p = (kaf.reshape(bn, s, N_HEADS, HEAD_DIM), vaf.reshape(bn, s, N_HEADS, HEAD_DIM),
             kwf[:, :IDX_DIM].reshape(bn, s, IDX_DIM),
             kbf.reshape(bn, s, N_HEADS, HEAD_DIM), vbf.reshape(bn, s, N_HEADS, HEAD_DIM))

    ms_ = db * nq
    cs_s, sn_s = _rope_tables(past + jnp.arange(nq))
    cs_s, sn_s = jnp.tile(cs_s, (db, 1)), jnp.tile(sn_s, (db, 1))
    (qa_s, qi_s, qb_s, kaf_s, vaf_s, kbf_s, vbf_s, kab_s, vab_s, kbb_s, vbb_s, ga_s, gb_s, kwf_s, kwb_s, _) = \
        _proj_call(x_s.reshape(ms_, d), g_attn, w_arr, cs_s, sn_s, gk, ms_)
    heads = lambda t: t.reshape(db, nq, N_HEADS, HEAD_DIM).transpose(0, 2, 1, 3)
    eye = jnp.eye(N_HEADS, dtype=BF16)
    blockdiag = lambda t: jnp.einsum("bhqd,hg->bhqgd", heads(t), eye).reshape(db, N_HEADS * nq, WIDTH)
    qi_rows = heads(qi_s).reshape(db, N_HEADS * nq, IDX_DIM)
    wi_s = kwf_s[:, IDX_DIM:IDX_DIM + N_IDX_HEADS].reshape(db, nq, N_IDX_HEADS).transpose(0, 2, 1)
    wcol = jnp.broadcast_to(wi_s.reshape(db, N_IDX_HEADS * nq, 1), (db, N_IDX_HEADS * nq, LANES))
    new3 = lambda t: _pad_rows(t.reshape(db, nq, t.shape[-1]), LANES)
    pa, pb = _sample_a_call(
        page_table, qi_rows, wcol, blockdiag(qa_s), blockdiag(qb_s),
        cache_idx_k, cache_ka.reshape(n_pool, PAGE_SIZE, WIDTH), cache_kb.reshape(n_pool, PAGE_SIZE, WIDTH),
        new3(kwb_s[:, :IDX_DIM]), new3(kab_s), new3(kbb_s), nq)
    ya_s, yb_s = _sample_b_call(
        page_table, pa, pb, cache_va.reshape(n_pool, PAGE_SIZE, WIDTH), cache_vb.reshape(n_pool, PAGE_SIZE, WIDTH),
        new3(vab_s), new3(vbb_s), nq)
    y_s = _merge_ffn_call(x_s.reshape(ms_, d), ya_s.reshape(ms_, WIDTH), yb_s.reshape(ms_, WIDTH), ga_s, gb_s,
                          wa, wb, wo, g_ffn, wg, wu, wd, g_final, ms_)
    new_s = (kaf_s.reshape(db, nq, N_HEADS, HEAD_DIM), vaf_s.reshape(db, nq, N_HEADS, HEAD_DIM),
             kwf_s[:, :IDX_DIM].reshape(db, nq, IDX_DIM),
             kbf_s.reshape(db, nq, N_HEADS, HEAD_DIM), vbf_s.reshape(db, nq, N_HEADS, HEAD_DIM))
    return y_p.reshape(bn, s, d), y_s.reshape(db, nq, d), new_p, new_s


def kernel(x_prompt, x_sample, cache_ka, cache_va, cache_idx_k, cache_kb, cache_vb, page_table,
           g_attn, w_in, g_idx_k, w_proj_a, w_proj_b, w_out, g_ffn, w_gate, w_up, w_down, g_final):
    depth = w_in.shape[0]
    assert depth == 1, "the fused final norm assumes a single layer"
    y_p, y_s, new_p, new_s = _layer(
        x_prompt, x_sample, cache_ka[0], cache_va[0], cache_idx_k[0], cache_kb[0], cache_vb[0], page_table,
        g_attn[0], w_in[0], g_idx_k[0], w_proj_a[0], w_proj_b[0], w_out[0], g_ffn[0],
        w_gate[0], w_up[0], w_down[0], g_final)
    stack = lambda t: t[None]
    return (y_p, y_s) + tuple(stack(t) for t in new_p) + tuple(stack(t) for t in new_s)
```

```python
import functools

import jax
import jax.numpy as jnp
from jax import lax
from jax.experimental import pallas as pl
from jax.experimental.pallas import tpu as pltpu

F32 = jnp.float32
BF16 = jnp.bfloat16
I32 = jnp.int32

HEAD_DIM = 64
N_HEADS = 8
WIDTH = N_HEADS * HEAD_DIM
IDX_DIM = 64
N_IDX_HEADS = 8
TOPK_KEYS = 256
MOBA_BLOCK = 256
MOBA_TOPK = 3
PAGE_SIZE = 128
ROPE_THETA = 10000.0
RMS_EPS = 1e-6
NEG = -1e30
LANES = 128
SUBLANES = 8
INT_MIN = -(2 ** 31)
VMEM_LIMIT = 56 * 1024 * 1024
Q_TILE = 256
DSA_CHUNK = 256
PROJ_ROWS = 512
FFN_ROWS = 256
PAGES_PER_STEP = 8

_NT = (((1,), (1,)), ((), ()))


def _dot_nt(a, b):
    return lax.dot_general(a, b, _NT, preferred_element_type=F32)


def _dot(a, b):
    return jnp.dot(a, b, preferred_element_type=F32)


def _head(h):
    return slice(h * HEAD_DIM, (h + 1) * HEAD_DIM)


def _params(*sem):
    return pltpu.CompilerParams(dimension_semantics=sem, vmem_limit_bytes=VMEM_LIMIT)


def _const_spec(shape):
    nd = len(shape)
    return pl.BlockSpec(shape, lambda *_: (0,) * nd, pipeline_mode=pl.Buffered(1))


def _rope_tables(pos):
    inv_freq = ROPE_THETA ** (-jnp.arange(0, HEAD_DIM, 2, dtype=F32) / HEAD_DIM)
    ang = pos.astype(F32)[:, None] * inv_freq[None, :]
    cos, sin = jnp.cos(ang), jnp.sin(ang)
    return (jnp.concatenate([cos, cos, cos, cos], axis=-1),
            jnp.concatenate([-sin, sin, -sin, sin], axis=-1))


def _proj_kernel(x_ref, g_ref, w_ref, cs_ref, sn_ref, gk_ref, *outs, transposed, va_tile, vb_tile):
    tm = x_ref.shape[0]
    x = x_ref[...]
    ms = jnp.mean(x * x, axis=-1, keepdims=True)
    xn = (x * lax.rsqrt(ms + RMS_EPS) * g_ref[...]).astype(BF16)
    cs = cs_ref[...]
    sn = sn_ref[...]
    lane = lax.broadcasted_iota(I32, (tm, LANES), 1)
    first_half = (lane & (HEAD_DIM // 2)) == 0

    def rope(p):
        swapped = jnp.where(first_half, pltpu.roll(p, LANES - HEAD_DIM // 2, 1),
                            pltpu.roll(p, HEAD_DIM // 2, 1))
        return p * cs + swapped * sn

    def slab(j, width=WIDTH):
        return _dot(xn, w_ref[:, j * WIDTH:j * WIDTH + width])

    def rope_slab(h):
        return jnp.concatenate(
            [rope(h[:, k * LANES:(k + 1) * LANES]) for k in range(WIDTH // LANES)], axis=1)

    qa, qi, qb = rope_slab(slab(0)), rope_slab(slab(1)), rope_slab(slab(2))
    ka, va, kb, vb = rope_slab(slab(3)), slab(4), rope_slab(slab(5)), slab(6)
    t = slab(11, LANES)
    is_key = lane < IDX_DIM
    kms = jnp.sum(jnp.where(is_key, t * t, 0.0), axis=-1, keepdims=True) * (1.0 / IDX_DIM)
    kn = rope(t * lax.rsqrt(kms + RMS_EPS) * gk_ref[...])
    kw = jnp.where(is_key, kn, jnp.where(lane < IDX_DIM + N_IDX_HEADS, t * (N_IDX_HEADS ** -0.5), 0.0))

    if transposed:
        (qat_ref, qit_ref, qbt_ref, kab_ref, kbb_ref, vat4_ref, vbt4_ref,
         kat_ref, vat_ref, kbt_ref, vbt_ref, ga_ref, gb_ref, kwb_ref, kwt_ref, kmean_ref) = outs
        qat_ref[0] = qa.T.astype(BF16)
        qit_ref[0] = qi.T.astype(BF16)
        qbt_ref[0] = qb.T.astype(BF16)
        kab_ref[...] = ka.astype(BF16)
        kbb_ref[...] = kb.astype(BF16)
        kat_ref[0] = ka.T
        kbt_ref[0] = kb.T
        va_t, vb_t = va.T, vb.T
        vat_ref[0] = va_t
        vbt_ref[0] = vb_t
        for r in range(tm // va_tile):
            vat4_ref[0, r] = va_t[:, r * va_tile:(r + 1) * va_tile].astype(BF16)
        for r in range(tm // vb_tile):
            vbt4_ref[0, r] = vb_t[:, r * vb_tile:(r + 1) * vb_tile].astype(BF16)
        kwb_ref[...] = kw.astype(BF16)
        kwt_ref[0] = kw.T
        for r in range(tm // MOBA_BLOCK):
            kmean_ref[0, r:r + 1, :] = jnp.mean(kb[r * MOBA_BLOCK:(r + 1) * MOBA_BLOCK], axis=0, keepdims=True)
    else:
        (qa_ref, qi_ref, qb_ref, kaf_ref, vaf_ref, kbf_ref, vbf_ref,
         kab_ref, vab_ref, kbb_ref, vbb_ref, ga_ref, gb_ref, kwf_ref, kwb_ref) = outs
        qa_ref[...] = qa.astype(BF16)
        qi_ref[...] = qi.astype(BF16)
        qb_ref[...] = qb.astype(BF16)
        for f_ref, b_ref, val in ((kaf_ref, kab_ref, ka), (vaf_ref, vab_ref, va),
                                  (kbf_ref, kbb_ref, kb), (vbf_ref, vbb_ref, vb)):
            f_ref[...] = val
            b_ref[...] = val.astype(BF16)
        kwf_ref[...] = kw
        kwb_ref[...] = kw.astype(BF16)
    ga_ref[:, :WIDTH] = slab(7)
    ga_ref[:, WIDTH:] = slab(8)
    gb_ref[:, :WIDTH] = slab(9)
    gb_ref[:, WIDTH:] = slab(10)


def _arrange_w_in(w_in):
    d = w_in.shape[0]
    o = [0]
    for s in (WIDTH, WIDTH, WIDTH, N_IDX_HEADS * IDX_DIM, IDX_DIM, N_IDX_HEADS, WIDTH, WIDTH, WIDTH, d, d):
        o.append(o[-1] + s)
    qa, ka, va, qi, ki, wi, qb, kb, vb, ga, gb = [w_in[:, o[i]:o[i + 1]] for i in range(11)]
    qscale = HEAD_DIM ** -0.5
    iscale = IDX_DIM ** -0.5
    pad = jnp.zeros((d, LANES - IDX_DIM - N_IDX_HEADS), w_in.dtype)
    return jnp.concatenate([qa * qscale, qi * iscale, qb * qscale, ka, va, kb, vb, ga, gb, ki, wi, pad],
                           axis=1).astype(BF16)


def _proj_prompt_call(x, g_attn, w_arr, cs, sn, gk, bn, s):
    m, d = x.shape
    tm = PROJ_ROWS
    spt = s // tm
    row = lambda w: pl.BlockSpec((tm, w), lambda i: (i, 0))
    tab = pl.BlockSpec((tm, LANES), lambda i: (i % spt, 0))
    col = lambda r: pl.BlockSpec((1, r, tm), lambda i: (i // spt, 0, i % spt))
    tile4 = lambda w: pl.BlockSpec((1, tm // w, WIDTH, w), lambda i: (i // spt, i % spt, 0, 0))
    t_shape = lambda r, dt: jax.ShapeDtypeStruct((bn, r, s), dt)
    out_shape = (
        [t_shape(WIDTH, BF16)] * 3
        + [jax.ShapeDtypeStruct((m, WIDTH), BF16)] * 2
        + [jax.ShapeDtypeStruct((bn, s // DSA_CHUNK, WIDTH, DSA_CHUNK), BF16),
           jax.ShapeDtypeStruct((bn, s // MOBA_BLOCK, WIDTH, MOBA_BLOCK), BF16)]
        + [t_shape(WIDTH, F32)] * 4
        + [jax.ShapeDtypeStruct((m, d), F32)] * 2
        + [jax.ShapeDtypeStruct((m, LANES), BF16), t_shape(LANES, F32),
           jax.ShapeDtypeStruct((m // tm, tm // MOBA_BLOCK, WIDTH), F32)])
    out_specs = ([col(WIDTH)] * 3 + [row(WIDTH)] * 2 + [tile4(DSA_CHUNK), tile4(MOBA_BLOCK)]
                 + [col(WIDTH)] * 4 + [row(d)] * 2 + [row(LANES), col(LANES),
                 pl.BlockSpec((1, tm // MOBA_BLOCK, WIDTH), lambda i: (i, 0, 0))])
    kern = functools.partial(_proj_kernel, transposed=True, va_tile=DSA_CHUNK, vb_tile=MOBA_BLOCK)
    return pl.pallas_call(
        kern,
        grid=(m // tm,),
        in_specs=[row(d), _const_spec((1, d)), _const_spec(w_arr.shape), tab, tab, _const_spec((1, LANES))],
        out_specs=out_specs,
        out_shape=out_shape,
        compiler_params=_params("parallel"),
        name="proj",
    )(x, g_attn.reshape(1, d), w_arr, cs, sn, gk)


def _proj_sample_call(x, g_attn, w_arr, cs, sn, gk):
    m, d = x.shape
    row = lambda w: pl.BlockSpec((m, w), lambda i: (0, 0))
    out_shape = ([jax.ShapeDtypeStruct((m, WIDTH), BF16)] * 3 + [jax.ShapeDtypeStruct((m, WIDTH), F32)] * 4
                 + [jax.ShapeDtypeStruct((m, WIDTH), BF16)] * 4 + [jax.ShapeDtypeStruct((m, d), F32)] * 2
                 + [jax.ShapeDtypeStruct((m, LANES), F32), jax.ShapeDtypeStruct((m, LANES), BF16)])
    kern = functools.partial(_proj_kernel, transposed=False, va_tile=0, vb_tile=0)
    return pl.pallas_call(
        kern,
        grid=(1,),
        in_specs=[row(d), _const_spec((1, d)), _const_spec(w_arr.shape), row(LANES), row(LANES),
                  _const_spec((1, LANES))],
        out_specs=[row(WIDTH)] * 11 + [row(d)] * 2 + [row(LANES)] * 2,
        out_shape=out_shape,
        compiler_params=_params("arbitrary"),
        name="proj_sample",
    )(x, g_attn.reshape(1, d), w_arr, cs, sn, gk)


def _key_to_float(key):
    return lax.bitcast_convert_type(key ^ ((key >> 31) & jnp.int32(0x7FFFFFFF)), F32)


def _select_threshold(count, state_shape, bc, k_sel, idx_bits, y_ref):
    kf = float(k_sel)
    rep = lambda v: jnp.broadcast_to(v, state_shape)

    def bis(t, key):
        cand = key + lax.shift_left(jnp.int32(1), 31 - t)
        cf = bc(_key_to_float(cand))
        return jnp.where(rep(count(lambda x, _: x >= cf)) >= kf, cand, key)

    thr = _key_to_float(lax.fori_loop(0, 32, bis, jnp.full(state_shape, INT_MIN, I32)))
    tb = bc(thr)
    c_ge = count(lambda x, _: x >= tb)
    y_ref[...] = jnp.full(state_shape, 2 ** 30, I32)

    @pl.when(jnp.max(c_ge) > kf)
    def _():
        need = rep(kf - count(lambda x, _: x > tb))

        def ybis(t, y):
            cand = y + lax.shift_left(jnp.int32(1), idx_bits - 1 - t)
            cb = bc(cand)
            f = count(lambda x, idx: (x == tb) & (idx < cb))
            return jnp.where(rep(f) < need, cand, y)

        y_ref[...] = lax.fori_loop(0, idx_bits, ybis, jnp.zeros(state_shape, I32))

    return thr


def _top_blocks(gate, valid, n_top, axis):
    n = gate.shape[axis]
    ids = lax.broadcasted_iota(I32, gate.shape, axis).astype(F32)
    g = jnp.where(valid, gate, NEG)
    marks = jnp.zeros(gate.shape, F32)
    for _ in range(n_top):
        mx = jnp.max(g, axis=axis, keepdims=True)
        first = jnp.min(jnp.where(g == mx, ids, float(n)), axis=axis, keepdims=True)
        hit = ids == first
        marks = jnp.where(hit, 1.0, marks)
        g = jnp.where(hit, -jnp.inf, g)
    return jnp.where(valid, marks, 0.0)


def _attend_chunk_t(kc, qt_ref, vt, bias, lg_ref, p_ref, m_ref, l_ref, acc_ref):
    for h in range(N_HEADS):
        lg_ref[h] = _dot(kc[:, _head(h)], qt_ref[0, _head(h), :]) + bias(h)
    for h in range(N_HEADS):
        lg = lg_ref[h]
        m_old = m_ref[h]
        m_new = jnp.maximum(m_old, jnp.max(lg, axis=0, keepdims=True))
        alpha = jnp.exp(m_old - m_new)
        p = jnp.exp(lg - m_new[0:1])
        l_ref[h] = alpha * l_ref[h] + jnp.sum(p, axis=0, keepdims=True)
        m_ref[h] = m_new
        p_ref[h] = p.astype(BF16)
        acc_ref[_head(h), :] = alpha[0:1] * acc_ref[_head(h), :]
    for h in range(N_HEADS):
        acc_ref[_head(h), :] += _dot(vt(h), p_ref[h])


def _init_softmax_state(m_ref, l_ref, acc_ref):
    m_ref[...] = jnp.full(m_ref.shape, NEG, F32)
    l_ref[...] = jnp.zeros(l_ref.shape, F32)
    acc_ref[...] = jnp.zeros(acc_ref.shape, F32)


def _finish_heads_t(o_ref, l_ref, acc_ref):
    for h in range(N_HEADS):
        acc_ref[_head(h), :] = acc_ref[_head(h), :] / l_ref[h][0:1]
    o_ref[...] = acc_ref[...].T.astype(o_ref.dtype)


def _attn_scratch(c_w, qt):
    return [pltpu.VMEM((N_HEADS, c_w, qt), F32), pltpu.VMEM((N_HEADS, c_w, qt), BF16),
            pltpu.VMEM((N_HEADS, SUBLANES, qt), F32), pltpu.VMEM((N_HEADS, SUBLANES, qt), F32),
            pltpu.VMEM((WIDTH, qt), F32)]


def _dsa_prompt_kernel(qat_ref, qit_ref, kwt_ref, kw_ref, ka_ref, vt_ref, o_ref,
                       sc_ref, y_ref, lg_ref, p_ref, m_ref, l_ref, acc_ref, *, k_sel, idx_bits):
    i = pl.program_id(1)
    _, c_w, qt = sc_ref.shape
    nkc = ((i + 1) * qt + c_w - 1) // c_w
    row = lax.broadcasted_iota(I32, (c_w, qt), 0)
    qpos = i * qt + lax.broadcasted_iota(I32, (c_w, qt), 1)

    def score_body(c, carry):
        k0 = pl.multiple_of(c * c_w, c_w)
        ki = kw_ref[pl.ds(k0, c_w), :][:, :IDX_DIM]
        acc = jnp.zeros((c_w, qt), F32)
        for h in range(N_IDX_HEADS):
            s = _dot(ki, qit_ref[0, h * IDX_DIM:(h + 1) * IDX_DIM, :])
            acc = acc + jnp.maximum(s, 0.0) * kwt_ref[0, IDX_DIM + h:IDX_DIM + h + 1, :]
        sc_ref[c] = jnp.where((k0 + row) <= qpos, acc, NEG)
        return carry

    lax.fori_loop(0, nkc, score_body, 0)

    def count(pred):
        def body(c, cnt):
            hit = jnp.where(pred(sc_ref[c], c * c_w + row), 1.0, 0.0)
            return cnt + jnp.sum(hit.reshape(c_w // SUBLANES, SUBLANES, qt), axis=0)

        cnt = lax.fori_loop(0, nkc, body, jnp.zeros((SUBLANES, qt), F32))
        return jnp.sum(cnt, axis=0, keepdims=True)

    thr = _select_threshold(count, (SUBLANES, qt), lambda v: v[0:1], k_sel, idx_bits, y_ref)
    tb = thr[0:1]
    yb = y_ref[0:1, :]
    _init_softmax_state(m_ref, l_ref, acc_ref)

    def att_body(c, carry):
        k0 = pl.multiple_of(c * c_w, c_w)
        x = sc_ref[c]
        kidx = k0 + row
        sel = ((x > tb) | ((x == tb) & (kidx <= yb))) & (kidx <= qpos)
        bias = jnp.where(sel, 0.0, NEG)
        _attend_chunk_t(ka_ref[pl.ds(k0, c_w), :], qat_ref, lambda h: vt_ref[0, c, _head(h), :],
                        lambda h: bias, lg_ref, p_ref, m_ref, l_ref, acc_ref)
        return carry

    lax.fori_loop(0, nkc, att_body, 0)
    _finish_heads_t(o_ref, l_ref, acc_ref)


def _dsa_prompt_call(qat, qit, kwt, kwb, kab, vat4, bn, s):
    qt, c_w = Q_TILE, DSA_CHUNK
    nq = s // qt
    k_sel = min(TOPK_KEYS, s // 4)
    idx_bits = int(s).bit_length()
    qcol = lambda r: pl.BlockSpec((1, r, qt), lambda b, i: (b, 0, i))
    seq = lambda w: pl.BlockSpec((s, w), lambda b, i: (b, 0))
    kern = functools.partial(_dsa_prompt_kernel, k_sel=k_sel, idx_bits=idx_bits)
    return pl.pallas_call(
        kern,
        grid=(bn, nq),
        in_specs=[qcol(WIDTH), qcol(WIDTH), qcol(LANES), seq(LANES), seq(WIDTH),
                  pl.BlockSpec((1, s // c_w, WIDTH, c_w), lambda b, i: (b, 0, 0, 0))],
        out_specs=pl.BlockSpec((qt, WIDTH), lambda b, i: (b * nq + i, 0)),
        out_shape=jax.ShapeDtypeStruct((bn * s, WIDTH), BF16),
        scratch_shapes=[pltpu.VMEM((s // c_w, c_w, qt), F32), pltpu.VMEM((SUBLANES, qt), I32)]
        + _attn_scratch(c_w, qt),
        compiler_params=_params("parallel", "arbitrary"),
        name="dsa_prompt",
    )(qat, qit, kwt, kwb, kab, vat4)


def _moba_prompt_kernel(qbt_ref, km_ref, kb_ref, vt_ref, o_ref, marks_ref, lg_ref, p_ref, m_ref, l_ref, acc_ref):
    cur = pl.program_id(1)
    nb, qt = marks_ref.shape[1:]
    km = km_ref[0].astype(BF16)
    past = lax.broadcasted_iota(I32, (nb, qt), 0) < cur
    for h in range(N_HEADS):
        marks_ref[h] = _top_blocks(_dot(km[:, _head(h)], qbt_ref[0, _head(h), :]), past, MOBA_TOPK, 0)
    _init_softmax_state(m_ref, l_ref, acc_ref)

    def block(j, bias):
        kc = kb_ref[pl.ds(pl.multiple_of(j * MOBA_BLOCK, MOBA_BLOCK), MOBA_BLOCK), :]
        _attend_chunk_t(kc, qbt_ref, lambda h: vt_ref[0, j, _head(h), :], bias,
                        lg_ref, p_ref, m_ref, l_ref, acc_ref)

    def blk_body(j, carry):
        block(j, lambda h: jnp.where(marks_ref[h, pl.ds(j, 1), :] > 0.5, 0.0, NEG))
        return carry

    lax.fori_loop(0, cur, blk_body, 0)
    causal = (lax.broadcasted_iota(I32, (MOBA_BLOCK, qt), 0) <= lax.broadcasted_iota(I32, (MOBA_BLOCK, qt), 1))
    own_bias = jnp.where(causal, 0.0, NEG)
    block(cur, lambda h: own_bias)
    _finish_heads_t(o_ref, l_ref, acc_ref)


def _moba_prompt_call(qbt, kmean, kbb, vbt4, bn, s):
    qt = MOBA_BLOCK
    nq = s // qt
    nb = kmean.shape[1]
    return pl.pallas_call(
        _moba_prompt_kernel,
        grid=(bn, nq),
        in_specs=[pl.BlockSpec((1, WIDTH, qt), lambda b, i: (b, 0, i)),
                  pl.BlockSpec((1, nb, WIDTH), lambda b, i: (b, 0, 0)),
                  pl.BlockSpec((s, WIDTH), lambda b, i: (b, 0)),
                  pl.BlockSpec((1, nq, WIDTH, qt), lambda b, i: (b, 0, 0, 0))],
        out_specs=pl.BlockSpec((qt, WIDTH), lambda b, i: (b * nq + i, 0)),
        out_shape=jax.ShapeDtypeStruct((bn * s, WIDTH), BF16),
        scratch_shapes=[pltpu.VMEM((N_HEADS, nb, qt), F32)] + _attn_scratch(MOBA_BLOCK, qt),
        compiler_params=_params("parallel", "arbitrary"),
        name="moba_prompt",
    )(qbt, kmean, kbb, vbt4)


def _sample_a_kernel(pt_ref, qi_ref, wcol_ref, qda_ref, qdb_ref, *refs, n_pages, nq, k_sel, idx_bits):
    del pt_ref
    pp = PAGES_PER_STEP
    idx_refs, ka_refs, kb_refs = refs[0:pp], refs[pp:2 * pp], refs[2 * pp:3 * pp]
    kin_ref, kan_ref, kbn_ref, pa_ref, pb_ref, sc_ref, y_ref, la_ref, lb_ref = refs[3 * pp:]
    p = pl.program_id(1)
    nch = n_pages + 1
    rows = N_HEADS * nq
    lane = lax.broadcasted_iota(I32, (rows, LANES), 1)
    qrow = lax.broadcasted_iota(I32, (rows, LANES), 0) % nq

    def idx_scores(k_t):
        s = jnp.maximum(_dot(qi_ref[0], k_t), 0.0) * wcol_ref[0]
        sc = s[0:nq]
        for h in range(1, N_IDX_HEADS):
            sc = sc + s[h * nq:(h + 1) * nq]
        return sc

    for t in range(pp):
        g = p * pp + t
        sc_ref[g] = idx_scores(idx_refs[t][0].astype(BF16))
        la_ref[g] = _dot(qda_ref[0], ka_refs[t][0].astype(BF16))
        lb_ref[g] = _dot(qdb_ref[0], kb_refs[t][0].astype(BF16))

    def softmax_chunks(l_ref, p_ref, mx):
        m = jnp.max(mx, axis=-1, keepdims=True)

        def exp_body(c, ssum):
            e = jnp.exp(l_ref[c] - m)
            l_ref[c] = e
            return ssum + e

        ssum = lax.fori_loop(0, nch, exp_body, jnp.zeros((rows, LANES), F32))
        inv = 1.0 / jnp.sum(ssum, axis=-1, keepdims=True)

        def out_body(c, carry):
            p_ref[0, c] = (l_ref[c] * inv).astype(p_ref.dtype)
            return carry

        lax.fori_loop(0, nch, out_body, 0)

    @pl.when(p == n_pages // pp - 1)
    def _():
        new_ok = (lane <= qrow) & (lane < nq)
        neg = jnp.full((rows, LANES), NEG, F32)
        sc_ref[n_pages] = jnp.where(new_ok[:nq], idx_scores(kin_ref[0]), NEG)

        def count(pred):
            def body(c, cnt):
                return cnt + jnp.where(pred(sc_ref[c], c * LANES + lane[:nq]), 1.0, 0.0)

            cnt = lax.fori_loop(0, nch, body, jnp.zeros((nq, LANES), F32))
            return jnp.sum(cnt, axis=-1, keepdims=True)

        thr = _select_threshold(count, (nq, LANES), lambda v: v, k_sel, idx_bits, y_ref)
        yv = y_ref[...]
        la_ref[n_pages] = jnp.where(new_ok, _dot(qda_ref[0], kan_ref[0]), NEG)

        def mask_a(c, mx):
            x = sc_ref[c]
            sel = (x > thr) | ((x == thr) & ((c * LANES + lane[:nq]) <= yv))
            lg = jnp.where(jnp.concatenate([sel] * N_HEADS, axis=0), la_ref[c], NEG)
            la_ref[c] = lg
            return jnp.maximum(mx, lg)

        softmax_chunks(la_ref, pa_ref, lax.fori_loop(0, nch, mask_a, neg))
        cpb = MOBA_BLOCK // PAGE_SIZE
        gate = jnp.zeros((rows, LANES), F32)
        for j in range(n_pages // cpb):
            tot = lb_ref[j * cpb]
            for c in range(1, cpb):
                tot = tot + lb_ref[j * cpb + c]
            gate = jnp.where(lane == j, jnp.sum(tot, axis=-1, keepdims=True) * (1.0 / MOBA_BLOCK), gate)
        marks = _top_blocks(gate, lane < n_pages // cpb, MOBA_TOPK, 1)

        def mask_b(c, mx):
            picked = jnp.sum(jnp.where(lane == c // cpb, marks, 0.0), axis=-1, keepdims=True) > 0.5
            lg = jnp.where(picked, lb_ref[c], NEG)
            lb_ref[c] = lg
            return jnp.maximum(mx, lg)

        mx_b = lax.fori_loop(0, n_pages, mask_b, neg)
        lg_new = jnp.where(new_ok, _dot(qdb_ref[0], kbn_ref[0]), NEG)
        lb_ref[n_pages] = lg_new
        softmax_chunks(lb_ref, pb_ref, jnp.maximum(mx_b, lg_new))


def _page_specs(rows):
    pp = PAGES_PER_STEP
    return [pl.BlockSpec((1, rows, PAGE_SIZE), functools.partial(
        lambda b, p, pt, t: (pt[b, p * pp + t], 0, 0), t=t)) for t in range(pp)]


def _sample_a_call(page_table, qi_rows, wcol, qda, qdb, c_idx, c_ka, c_kb, ki_new, ka_new, kb_new, nq):
    bn, n_pages = page_table.shape
    pp = PAGES_PER_STEP
    assert n_pages % pp == 0 and MOBA_BLOCK % PAGE_SIZE == 0
    rows = N_HEADS * nq
    nch = n_pages + 1
    k_sel = min(TOPK_KEYS, (n_pages * PAGE_SIZE + nq) // 4)
    idx_bits = int(nch * LANES).bit_length()
    per_b = lambda shp: pl.BlockSpec((1,) + shp, lambda b, p, pt: (b,) + (0,) * len(shp))
    kern = functools.partial(_sample_a_kernel, n_pages=n_pages, nq=nq, k_sel=k_sel, idx_bits=idx_bits)
    grid_spec = pltpu.PrefetchScalarGridSpec(
        num_scalar_prefetch=1,
        grid=(bn, n_pages // pp),
        in_specs=[per_b((rows, IDX_DIM)), per_b((rows, LANES)), per_b((rows, WIDTH)), per_b((rows, WIDTH))]
        + _page_specs(IDX_DIM) + _page_specs(WIDTH) + _page_specs(WIDTH)
        + [per_b((IDX_DIM, LANES)), per_b((WIDTH, LANES)), per_b((WIDTH, LANES))],
        out_specs=[per_b((nch, rows, LANES)), per_b((nch, rows, LANES))],
        scratch_shapes=[
            pltpu.VMEM((nch, nq, LANES), F32),
            pltpu.VMEM((nq, LANES), I32),
            pltpu.VMEM((nch, rows, LANES), F32),
            pltpu.VMEM((nch, rows, LANES), F32),
        ],
    )
    return pl.pallas_call(
        kern,
        grid_spec=grid_spec,
        out_shape=[jax.ShapeDtypeStruct((bn, nch, rows, LANES), BF16)] * 2,
        compiler_params=_params("parallel", "arbitrary"),
        name="sample_a",
    )(page_table, qi_rows, wcol, qda, qdb, *([c_idx] * pp), *([c_ka] * pp), *([c_kb] * pp),
      ki_new, ka_new, kb_new)


def _sample_b_kernel(pt_ref, pa_ref, pb_ref, pan_ref, pbn_ref, *refs, n_pages, nq):
    del pt_ref
    pp = PAGES_PER_STEP
    va_refs, vb_refs = refs[0:pp], refs[pp:2 * pp]
    van_ref, vbn_ref, oa_ref, ob_ref, acca_ref, accb_ref = refs[2 * pp:]
    p = pl.program_id(1)

    @pl.when(p == 0)
    def _():
        acca_ref[...] = _dot_nt(pan_ref[0, 0], van_ref[0])
        accb_ref[...] = _dot_nt(pbn_ref[0, 0], vbn_ref[0])

    for t in range(pp):
        acca_ref[...] += _dot_nt(pa_ref[0, t], va_refs[t][0].astype(BF16))
        accb_ref[...] += _dot_nt(pb_ref[0, t], vb_refs[t][0].astype(BF16))

    @pl.when(p == n_pages // pp - 1)
    def _():
        head_of_lane = lax.broadcasted_iota(I32, (nq, WIDTH), 1) // HEAD_DIM
        for acc_ref, o_ref in ((acca_ref, oa_ref), (accb_ref, ob_ref)):
            out = jnp.zeros((nq, WIDTH), F32)
            for h in range(N_HEADS):
                out = out + jnp.where(head_of_lane == h, acc_ref[h * nq:(h + 1) * nq, :], 0.0)
            o_ref[0] = out.astype(o_ref.dtype)


def _sample_b_call(page_table, pa, pb, c_va, c_vb, va_new, vb_new, nq):
    bn, n_pages = page_table.shape
    pp = PAGES_PER_STEP
    rows = N_HEADS * nq
    per_b = lambda shp: pl.BlockSpec((1,) + shp, lambda b, p, pt: (b,) + (0,) * len(shp))
    pcol = pl.BlockSpec((1, pp, rows, LANES), lambda b, p, pt: (b, p, 0, 0))
    pnew = pl.BlockSpec((1, 1, rows, LANES), lambda b, p, pt: (b, n_pages, 0, 0))
    kern = functools.partial(_sample_b_kernel, n_pages=n_pages, nq=nq)
    grid_spec = pltpu.PrefetchScalarGridSpec(
        num_scalar_prefetch=1,
        grid=(bn, n_pages // pp),
        in_specs=[pcol, pcol, pnew, pnew] + _page_specs(WIDTH) + _page_specs(WIDTH)
        + [per_b((WIDTH, LANES)), per_b((WIDTH, LANES))],
        out_specs=[per_b((nq, WIDTH)), per_b((nq, WIDTH))],
        scratch_shapes=[pltpu.VMEM((rows, WIDTH), F32), pltpu.VMEM((rows, WIDTH), F32)],
    )
    return pl.pallas_call(
        kern,
        grid_spec=grid_spec,
        out_shape=[jax.ShapeDtypeStruct((bn, nq, WIDTH), BF16)] * 2,
        compiler_params=_params("parallel", "arbitrary"),
        name="sample_b",
    )(page_table, pa, pb, pa, pb, *([c_va] * pp), *([c_vb] * pp), va_new, vb_new)


def _merge_ffn_kernel(x_ref, ya_ref, yb_ref, ga_ref, gb_ref, wa_ref, wb_ref, wo_ref, gf_ref,
                      wg_ref, wu_ref, wd_ref, gfin_ref, y_ref, *, ff_slab):
    za = _dot(ya_ref[...], wa_ref[...])
    zb = _dot(yb_ref[...], wb_ref[...])
    mix = jax.nn.sigmoid(ga_ref[...]) * za + jax.nn.sigmoid(gb_ref[...]) * zb
    x1 = x_ref[...] + _dot(mix.astype(BF16), wo_ref[...])
    ms = jnp.mean(x1 * x1, axis=-1, keepdims=True)
    xn = (x1 * lax.rsqrt(ms + RMS_EPS) * gf_ref[...]).astype(BF16)
    x2 = x1
    d_ff = wg_ref.shape[1]
    for s0 in range(0, d_ff, ff_slab):
        hg = _dot(xn, wg_ref[:, s0:s0 + ff_slab])
        hu = _dot(xn, wu_ref[:, s0:s0 + ff_slab])
        act = (hg * jax.nn.sigmoid(hg) * hu).astype(BF16)
        x2 = x2 + _dot(act, wd_ref[s0:s0 + ff_slab, :])
    ms2 = jnp.mean(x2 * x2, axis=-1, keepdims=True)
    y_ref[...] = x2 * lax.rsqrt(ms2 + RMS_EPS) * gfin_ref[...]


def _merge_ffn_call(x, ya, yb, ga, gb, wa, wb, wo, gf, wg, wu, wd, gfin, tm):
    m, d = x.shape
    d_ff = wg.shape[1]
    ff_slab = d_ff // 2 if (d_ff // 2) % LANES == 0 else d_ff
    row = lambda w: pl.BlockSpec((tm, w), lambda i: (i, 0))
    kern = functools.partial(_merge_ffn_kernel, ff_slab=ff_slab)
    return pl.pallas_call(
        kern,
        grid=(m // tm,),
        in_specs=[row(d), row(WIDTH), row(WIDTH), row(d), row(d),
                  _const_spec(wa.shape), _const_spec(wb.shape), _const_spec(wo.shape), _const_spec((1, d)),
                  _const_spec(wg.shape), _const_spec(wu.shape), _const_spec(wd.shape), _const_spec((1, d))],
        out_specs=row(d),
        out_shape=jax.ShapeDtypeStruct((m, d), F32),
        compiler_params=_params("parallel"),
        name="merge_ffn",
    )(x, ya, yb, ga, gb, wa, wb, wo, gf.reshape(1, d), wg, wu, wd, gfin.reshape(1, d))


def _layer(x_p, x_s, cache_ka, cache_va, cache_idx_k, cache_kb, cache_vb, page_table,
           g_attn, w_in, g_idx_k, w_proj_a, w_proj_b, w_out, g_ffn, w_gate, w_up, w_down, g_final):
    bn, s, d = x_p.shape
    db, nq, _ = x_s.shape
    n_pages = page_table.shape[1]
    n_pool = cache_ka.shape[0]
    past = n_pages * PAGE_SIZE
    assert s % PROJ_ROWS == 0 and Q_TILE == MOBA_BLOCK and PROJ_ROWS % MOBA_BLOCK == 0

    w_arr = _arrange_w_in(w_in)
    gk = jnp.pad(g_idx_k, (0, LANES - IDX_DIM)).reshape(1, LANES)
    wa, wb, wo = w_proj_a.astype(BF16), w_proj_b.astype(BF16), w_out.astype(BF16)
    wg, wu, wd = w_gate.astype(BF16), w_up.astype(BF16), w_down.astype(BF16)

    cs_p, sn_p = _rope_tables(jnp.arange(s))
    (qat, qit, qbt, kab, kbb, vat4, vbt4, kat, vat, kbt, vbt, ga, gb, kwb, kwt, kmean) = _proj_prompt_call(
        x_p.reshape(bn * s, d), g_attn, w_arr, cs_p, sn_p, gk, bn, s)
    ya = _dsa_prompt_call(qat, qit, kwt, kwb, kab, vat4, bn, s)
    nb = s // MOBA_BLOCK
    kmean_b = jnp.pad(kmean.reshape(bn, nb, WIDTH), ((0, 0), (0, -nb % SUBLANES), (0, 0)))
    yb = _moba_prompt_call(qbt, kmean_b, kbb, vbt4, bn, s)
    y_p = _merge_ffn_call(x_p.reshape(bn * s, d), ya, yb, ga, gb, wa, wb, wo, g_ffn, wg, wu, wd, g_final,
                          FFN_ROWS)
    heads_out = lambda t: t.reshape(bn, N_HEADS, HEAD_DIM, s).transpose(0, 3, 1, 2)
    new_p = (heads_out(kat), heads_out(vat), kwt[:, :IDX_DIM, :].transpose(0, 2, 1), heads_out(kbt), heads_out(vbt))

    ms_ = db * nq
    cs_s, sn_s = _rope_tables(past + jnp.arange(nq))
    cs_s, sn_s = jnp.tile(cs_s, (db, 1)), jnp.tile(sn_s, (db, 1))
    (qa_s, qi_s, qb_s, kaf_s, vaf_s, kbf_s, vbf_s, kab_s, vab_s, kbb_s, vbb_s, ga_s, gb_s, kwf_s, kwb_s) = \
        _proj_sample_call(x_s.reshape(ms_, d), g_attn, w_arr, cs_s, sn_s, gk)
    heads = lambda t: t.reshape(db, nq, N_HEADS, HEAD_DIM).transpose(0, 2, 1, 3)
    eye = jnp.eye(N_HEADS, dtype=BF16)
    blockdiag = lambda t: jnp.einsum("bhqd,hg->bhqgd", heads(t), eye).reshape(db, N_HEADS * nq, WIDTH)
    qi_rows = heads(qi_s).reshape(db, N_HEADS * nq, IDX_DIM)
    wi_s = kwf_s[:, IDX_DIM:IDX_DIM + N_IDX_HEADS].reshape(db, nq, N_IDX_HEADS).transpose(0, 2, 1)
    wcol = jnp.broadcast_to(wi_s.reshape(db, N_IDX_HEADS * nq, 1), (db, N_IDX_HEADS * nq, LANES))
    new_t = lambda t: jnp.pad(t.reshape(db, nq, t.shape[-1]).transpose(0, 2, 1), ((0, 0), (0, 0), (0, LANES - nq)))
    page_t = lambda c: c.transpose(0, 2, 3, 1).reshape(n_pool, WIDTH, PAGE_SIZE)
    pa, pb = _sample_a_call(
        page_table, qi_rows, wcol, blockdiag(qa_s), blockdiag(qb_s),
        cache_idx_k.transpose(0, 2, 1), page_t(cache_ka), page_t(cache_kb),
        new_t(kwb_s[:, :IDX_DIM]), new_t(kab_s), new_t(kbb_s), nq)
    ya_s, yb_s = _sample_b_call(page_table, pa, pb, page_t(cache_va), page_t(cache_vb),
                                new_t(vab_s), new_t(vbb_s), nq)
    y_s = _merge_ffn_call(x_s.reshape(ms_, d), ya_s.reshape(ms_, WIDTH), yb_s.reshape(ms_, WIDTH), ga_s, gb_s,
                          wa, wb, wo, g_ffn, wg, wu, wd, g_final, ms_)
    new_s = (kaf_s.reshape(db, nq, N_HEADS, HEAD_DIM), vaf_s.reshape(db, nq, N_HEADS, HEAD_DIM),
             kwf_s[:, :IDX_DIM].reshape(db, nq, IDX_DIM),
             kbf_s.reshape(db, nq, N_HEADS, HEAD_DIM), vbf_s.reshape(db, nq, N_HEADS, HEAD_DIM))
    return y_p.reshape(bn, s, d), y_s.reshape(db, nq, d), new_p, new_s


def kernel(x_prompt, x_sample, cache_ka, cache_va, cache_idx_k, cache_kb, cache_vb, page_table,
           g_attn, w_in, g_idx_k, w_proj_a, w_proj_b, w_out, g_ffn, w_gate, w_up, w_down, g_final):
    assert w_in.shape[0] == 1, "the fused final norm assumes a single layer"
    y_p, y_s, new_p, new_s = _layer(
        x_prompt, x_sample, cache_ka[0], cache_va[0], cache_idx_k[0], cache_kb[0], cache_vb[0], page_table,
        g_attn[0], w_in[0], g_idx_k[0], w_proj_a[0], w_proj_b[0], w_out[0], g_ffn[0],
        w_gate[0], w_up[0], w_down[0], g_final)
    return (y_p, y_s) + tuple(t[None] for t in new_p) + tuple(t[None] for t in new_s)
```

```python
import functools

import jax
import jax.numpy as jnp
from jax import lax
from jax.experimental import pallas as pl
from jax.experimental.pallas import tpu as pltpu

F32 = jnp.float32
BF16 = jnp.bfloat16
I32 = jnp.int32

HEAD_DIM = 64
N_HEADS = 8
WIDTH = N_HEADS * HEAD_DIM
IDX_DIM = 64
N_IDX_HEADS = 8
TOPK_KEYS = 256
MOBA_BLOCK = 256
MOBA_TOPK = 3
PAGE_SIZE = 128
ROPE_THETA = 10000.0
RMS_EPS = 1e-6
NEG = -1e30
LANES = 128
SUBLANES = 8
INT_MIN = -(2 ** 31)
VMEM_LIMIT = 56 * 1024 * 1024
Q_TILE = 256
DSA_CHUNK = 256
PROJ_ROWS = 512
FFN_ROWS = 256
PAGES_PER_STEP = 8

_NT = (((1,), (1,)), ((), ()))


def _dot_nt(a, b):
    return lax.dot_general(a, b, _NT, preferred_element_type=F32)


def _dot(a, b):
    return jnp.dot(a, b, preferred_element_type=F32)


def _head(h):
    return slice(h * HEAD_DIM, (h + 1) * HEAD_DIM)


def _params(*sem):
    return pltpu.CompilerParams(dimension_semantics=sem, vmem_limit_bytes=VMEM_LIMIT)


def _const_spec(shape):
    nd = len(shape)
    return pl.BlockSpec(shape, lambda *_: (0,) * nd, pipeline_mode=pl.Buffered(1))


def _rope_tables(pos):
    inv_freq = ROPE_THETA ** (-jnp.arange(0, HEAD_DIM, 2, dtype=F32) / HEAD_DIM)
    ang = pos.astype(F32)[:, None] * inv_freq[None, :]
    cos, sin = jnp.cos(ang), jnp.sin(ang)
    return (jnp.concatenate([cos, cos, cos, cos], axis=-1),
            jnp.concatenate([-sin, sin, -sin, sin], axis=-1))


def _proj_kernel(x_ref, g_ref, w_ref, cs_ref, sn_ref, gk_ref, *outs, transposed, va_tile, vb_tile):
    tm = x_ref.shape[0]
    x = x_ref[...]
    ms = jnp.mean(x * x, axis=-1, keepdims=True)
    xn = (x * lax.rsqrt(ms + RMS_EPS) * g_ref[...]).astype(BF16)
    cs = cs_ref[...]
    sn = sn_ref[...]
    lane = lax.broadcasted_iota(I32, (tm, LANES), 1)
    first_half = (lane & (HEAD_DIM // 2)) == 0

    def rope(p):
        swapped = jnp.where(first_half, pltpu.roll(p, LANES - HEAD_DIM // 2, 1),
                            pltpu.roll(p, HEAD_DIM // 2, 1))
        return p * cs + swapped * sn

    def slab(j, width=WIDTH):
        return _dot(xn, w_ref[:, j * WIDTH:j * WIDTH + width])

    def rope_slab(h):
        return jnp.concatenate(
            [rope(h[:, k * LANES:(k + 1) * LANES]) for k in range(WIDTH // LANES)], axis=1)

    qa, qi, qb = rope_slab(slab(0)), rope_slab(slab(1)), rope_slab(slab(2))
    ka, va, kb, vb = rope_slab(slab(3)), slab(4), rope_slab(slab(5)), slab(6)
    t = slab(11, LANES)
    is_key = lane < IDX_DIM
    kms = jnp.sum(jnp.where(is_key, t * t, 0.0), axis=-1, keepdims=True) * (1.0 / IDX_DIM)
    kn = rope(t * lax.rsqrt(kms + RMS_EPS) * gk_ref[...])
    kw = jnp.where(is_key, kn, jnp.where(lane < IDX_DIM + N_IDX_HEADS, t * (N_IDX_HEADS ** -0.5), 0.0))

    if transposed:
        (qat_ref, qit_ref, qbt_ref, kab_ref, kbb_ref, vat4_ref, vbt4_ref,
         kat_ref, vat_ref, kbt_ref, vbt_ref, ga_ref, gb_ref, kwb_ref, kwt_ref, kmean_ref) = outs
        qat_ref[0] = qa.T.astype(BF16)
        qit_ref[0] = qi.T.astype(BF16)
        qbt_ref[0] = qb.T.astype(BF16)
        kab_ref[...] = ka.astype(BF16)
        kbb_ref[...] = kb.astype(BF16)
        kat_ref[0] = ka.T
        kbt_ref[0] = kb.T
        va_t, vb_t = va.T, vb.T
        vat_ref[0] = va_t
        vbt_ref[0] = vb_t
        for r in range(tm // va_tile):
            vat4_ref[0, r] = va_t[:, r * va_tile:(r + 1) * va_tile].astype(BF16)
        for r in range(tm // vb_tile):
            vbt4_ref[0, r] = vb_t[:, r * vb_tile:(r + 1) * vb_tile].astype(BF16)
        kwb_ref[...] = kw.astype(BF16)
        kwt_ref[0] = kw.T
        for r in range(tm // MOBA_BLOCK):
            kmean_ref[0, r:r + 1, :] = jnp.mean(kb[r * MOBA_BLOCK:(r + 1) * MOBA_BLOCK], axis=0, keepdims=True)
    else:
        (qa_ref, qi_ref, qb_ref, kaf_ref, vaf_ref, kbf_ref, vbf_ref,
         kab_ref, vab_ref, kbb_ref, vbb_ref, ga_ref, gb_ref, kwf_ref, kwb_ref) = outs
        qa_ref[...] = qa.astype(BF16)
        qi_ref[...] = qi.astype(BF16)
        qb_ref[...] = qb.astype(BF16)
        for f_ref, b_ref, val in ((kaf_ref, kab_ref, ka), (vaf_ref, vab_ref, va),
                                  (kbf_ref, kbb_ref, kb), (vbf_ref, vbb_ref, vb)):
            f_ref[...] = val
            b_ref[...] = val.astype(BF16)
        kwf_ref[...] = kw
        kwb_ref[...] = kw.astype(BF16)
    ga_ref[:, :WIDTH] = slab(7)
    ga_ref[:, WIDTH:] = slab(8)
    gb_ref[:, :WIDTH] = slab(9)
    gb_ref[:, WIDTH:] = slab(10)


def _arrange_w_in(w_in):
    d = w_in.shape[0]
    o = [0]
    for s in (WIDTH, WIDTH, WIDTH, N_IDX_HEADS * IDX_DIM, IDX_DIM, N_IDX_HEADS, WIDTH, WIDTH, WIDTH, d, d):
        o.append(o[-1] + s)
    qa, ka, va, qi, ki, wi, qb, kb, vb, ga, gb = [w_in[:, o[i]:o[i + 1]] for i in range(11)]
    qscale = HEAD_DIM ** -0.5
    iscale = IDX_DIM ** -0.5
    pad = jnp.zeros((d, LANES - IDX_DIM - N_IDX_HEADS), w_in.dtype)
    return jnp.concatenate([qa * qscale, qi * iscale, qb * qscale, ka, va, kb, vb, ga, gb, ki, wi, pad],
                           axis=1).astype(BF16)


def _proj_prompt_call(x, g_attn, w_arr, cs, sn, gk, bn, s):
    m, d = x.shape
    tm = PROJ_ROWS
    spt = s // tm
    row = lambda w: pl.BlockSpec((tm, w), lambda i: (i, 0))
    tab = pl.BlockSpec((tm, LANES), lambda i: (i % spt, 0))
    col = lambda r: pl.BlockSpec((1, r, tm), lambda i: (i // spt, 0, i % spt))
    tile4 = lambda w: pl.BlockSpec((1, tm // w, WIDTH, w), lambda i: (i // spt, i % spt, 0, 0))
    t_shape = lambda r, dt: jax.ShapeDtypeStruct((bn, r, s), dt)
    out_shape = (
        [t_shape(WIDTH, BF16)] * 3
        + [jax.ShapeDtypeStruct((m, WIDTH), BF16)] * 2
        + [jax.ShapeDtypeStruct((bn, s // DSA_CHUNK, WIDTH, DSA_CHUNK), BF16),
           jax.ShapeDtypeStruct((bn, s // MOBA_BLOCK, WIDTH, MOBA_BLOCK), BF16)]
        + [t_shape(WIDTH, F32)] * 4
        + [jax.ShapeDtypeStruct((m, d), F32)] * 2
        + [jax.ShapeDtypeStruct((m, LANES), BF16), t_shape(LANES, F32),
           jax.ShapeDtypeStruct((m // tm, tm // MOBA_BLOCK, WIDTH), F32)])
    out_specs = ([col(WIDTH)] * 3 + [row(WIDTH)] * 2 + [tile4(DSA_CHUNK), tile4(MOBA_BLOCK)]
                 + [col(WIDTH)] * 4 + [row(d)] * 2 + [row(LANES), col(LANES),
                 pl.BlockSpec((1, tm // MOBA_BLOCK, WIDTH), lambda i: (i, 0, 0))])
    kern = functools.partial(_proj_kernel, transposed=True, va_tile=DSA_CHUNK, vb_tile=MOBA_BLOCK)
    return pl.pallas_call(
        kern,
        grid=(m // tm,),
        in_specs=[row(d), _const_spec((1, d)), _const_spec(w_arr.shape), tab, tab, _const_spec((1, LANES))],
        out_specs=out_specs,
        out_shape=out_shape,
        compiler_params=_params("parallel"),
        name="proj",
    )(x, g_attn.reshape(1, d), w_arr, cs, sn, gk)


def _proj_sample_call(x, g_attn, w_arr, cs, sn, gk):
    m, d = x.shape
    row = lambda w: pl.BlockSpec((m, w), lambda i: (0, 0))
    out_shape = ([jax.ShapeDtypeStruct((m, WIDTH), BF16)] * 3 + [jax.ShapeDtypeStruct((m, WIDTH), F32)] * 4
                 + [jax.ShapeDtypeStruct((m, WIDTH), BF16)] * 4 + [jax.ShapeDtypeStruct((m, d), F32)] * 2
                 + [jax.ShapeDtypeStruct((m, LANES), F32), jax.ShapeDtypeStruct((m, LANES), BF16)])
    kern = functools.partial(_proj_kernel, transposed=False, va_tile=0, vb_tile=0)
    return pl.pallas_call(
        kern,
        grid=(1,),
        in_specs=[row(d), _const_spec((1, d)), _const_spec(w_arr.shape), row(LANES), row(LANES),
                  _const_spec((1, LANES))],
        out_specs=[row(WIDTH)] * 11 + [row(d)] * 2 + [row(LANES)] * 2,
        out_shape=out_shape,
        compiler_params=_params("arbitrary"),
        name="proj_sample",
    )(x, g_attn.reshape(1, d), w_arr, cs, sn, gk)


def _key_to_float(key):
    return lax.bitcast_convert_type(key ^ ((key >> 31) & jnp.int32(0x7FFFFFFF)), F32)


def _select_threshold(count, state_shape, bc, k_sel, idx_bits, y_ref):
    kf = float(k_sel)
    rep = lambda v: jnp.broadcast_to(v, state_shape)

    def bis(t, key):
        cand = key + lax.shift_left(jnp.int32(1), 31 - t)
        cf = bc(_key_to_float(cand))
        return jnp.where(rep(count(lambda x, _: x >= cf)) >= kf, cand, key)

    thr = _key_to_float(lax.fori_loop(0, 32, bis, jnp.full(state_shape, INT_MIN, I32)))
    tb = bc(thr)
    c_ge = count(lambda x, _: x >= tb)
    y_ref[...] = jnp.full(state_shape, 2 ** 30, I32)

    @pl.when(jnp.max(c_ge) > kf)
    def _():
        need = rep(kf - count(lambda x, _: x > tb))

        def ybis(t, y):
            cand = y + lax.shift_left(jnp.int32(1), idx_bits - 1 - t)
            cb = bc(cand)
            f = count(lambda x, idx: (x == tb) & (idx < cb))
            return jnp.where(rep(f) < need, cand, y)

        y_ref[...] = lax.fori_loop(0, idx_bits, ybis, jnp.zeros(state_shape, I32))

    return thr


def _top_blocks(gate, valid, n_top, axis):
    n = gate.shape[axis]
    ids = lax.broadcasted_iota(I32, gate.shape, axis).astype(F32)
    g = jnp.where(valid, gate, NEG)
    marks = jnp.zeros(gate.shape, F32)
    for _ in range(n_top):
        mx = jnp.max(g, axis=axis, keepdims=True)
        first = jnp.min(jnp.where(g == mx, ids, float(n)), axis=axis, keepdims=True)
        hit = ids == first
        marks = jnp.where(hit, 1.0, marks)
        g = jnp.where(hit, -jnp.inf, g)
    return jnp.where(valid, marks, 0.0)


def _attend_chunk_t(kc, qt_ref, vt, bias, lg_ref, p_ref, m_ref, l_ref, acc_ref):
    for h in range(N_HEADS):
        lg_ref[h] = _dot(kc[:, _head(h)], qt_ref[0, _head(h), :]) + bias(h)
    for h in range(N_HEADS):
        lg = lg_ref[h]
        m_old = m_ref[h]
        m_new = jnp.maximum(m_old, jnp.max(lg, axis=0, keepdims=True))
        alpha = jnp.exp(m_old - m_new)
        p = jnp.exp(lg - m_new[0:1])
        l_ref[h] = alpha * l_ref[h] + jnp.sum(p, axis=0, keepdims=True)
        m_ref[h] = m_new
        p_ref[h] = p.astype(BF16)
        acc_ref[_head(h), :] = alpha[0:1] * acc_ref[_head(h), :]
    for h in range(N_HEADS):
        acc_ref[_head(h), :] += _dot(vt(h), p_ref[h])


def _init_softmax_state(m_ref, l_ref, acc_ref):
    m_ref[...] = jnp.full(m_ref.shape, NEG, F32)
    l_ref[...] = jnp.zeros(l_ref.shape, F32)
    acc_ref[...] = jnp.zeros(acc_ref.shape, F32)


def _finish_heads_t(o_ref, l_ref, acc_ref):
    for h in range(N_HEADS):
        acc_ref[_head(h), :] = acc_ref[_head(h), :] / l_ref[h][0:1]
    o_ref[...] = acc_ref[...].T.astype(o_ref.dtype)


def _attn_scratch(c_w, qt):
    return [pltpu.VMEM((N_HEADS, c_w, qt), F32), pltpu.VMEM((N_HEADS, c_w, qt), BF16),
            pltpu.VMEM((N_HEADS, SUBLANES, qt), F32), pltpu.VMEM((N_HEADS, SUBLANES, qt), F32),
            pltpu.VMEM((WIDTH, qt), F32)]


def _dsa_prompt_kernel(qat_ref, qit_ref, kwt_ref, kw_ref, ka_ref, vt_ref, o_ref,
                       sc_ref, y_ref, lg_ref, p_ref, m_ref, l_ref, acc_ref, *, k_sel, idx_bits):
    i = pl.program_id(1)
    _, c_w, qt = sc_ref.shape
    nkc = ((i + 1) * qt + c_w - 1) // c_w
    row = lax.broadcasted_iota(I32, (c_w, qt), 0)
    qpos = i * qt + lax.broadcasted_iota(I32, (c_w, qt), 1)

    def score_body(c, carry):
        k0 = pl.multiple_of(c * c_w, c_w)
        ki = kw_ref[pl.ds(k0, c_w), :][:, :IDX_DIM]
        acc = jnp.zeros((c_w, qt), F32)
        for h in range(N_IDX_HEADS):
            s = _dot(ki, qit_ref[0, h * IDX_DIM:(h + 1) * IDX_DIM, :])
            acc = acc + jnp.maximum(s, 0.0) * kwt_ref[0, IDX_DIM + h:IDX_DIM + h + 1, :]
        sc_ref[c] = jnp.where((k0 + row) <= qpos, acc, NEG)
        return carry

    lax.fori_loop(0, nkc, score_body, 0)

    @pl.when(nkc % 2 == 1)
    def _():
        sc_ref[nkc] = jnp.full((c_w, qt), NEG, F32)

    def count(pred):
        def body(c2, cnt):
            for c in (2 * c2, 2 * c2 + 1):
                hit = jnp.where(pred(sc_ref[c], c * c_w + row), 1.0, 0.0)
                cnt = cnt + jnp.sum(hit.reshape(c_w // SUBLANES, SUBLANES, qt), axis=0)
            return cnt

        cnt = lax.fori_loop(0, (nkc + 1) // 2, body, jnp.zeros((SUBLANES, qt), F32))
        return jnp.sum(cnt, axis=0, keepdims=True)

    thr = _select_threshold(count, (SUBLANES, qt), lambda v: v[0:1], k_sel, idx_bits, y_ref)
    tb = thr[0:1]
    yb = y_ref[0:1, :]
    _init_softmax_state(m_ref, l_ref, acc_ref)

    def att_body(c, carry):
        k0 = pl.multiple_of(c * c_w, c_w)
        x = sc_ref[c]
        kidx = k0 + row
        sel = ((x > tb) | ((x == tb) & (kidx <= yb))) & (kidx <= qpos)
        bias = jnp.where(sel, 0.0, NEG)
        _attend_chunk_t(ka_ref[pl.ds(k0, c_w), :], qat_ref, lambda h: vt_ref[0, c, _head(h), :],
                        lambda h: bias, lg_ref, p_ref, m_ref, l_ref, acc_ref)
        return carry

    lax.fori_loop(0, nkc, att_body, 0)
    _finish_heads_t(o_ref, l_ref, acc_ref)


def _dsa_prompt_call(qat, qit, kwt, kwb, kab, vat4, bn, s):
    qt, c_w = Q_TILE, DSA_CHUNK
    assert (s // c_w) % 2 == 0
    nq = s // qt
    k_sel = min(TOPK_KEYS, s // 4)
    idx_bits = int(s).bit_length()
    qcol = lambda r: pl.BlockSpec((1, r, qt), lambda b, i: (b, 0, i))
    seq = lambda w: pl.BlockSpec((s, w), lambda b, i: (b, 0))
    kern = functools.partial(_dsa_prompt_kernel, k_sel=k_sel, idx_bits=idx_bits)
    return pl.pallas_call(
        kern,
        grid=(bn, nq),
        in_specs=[qcol(WIDTH), qcol(WIDTH), qcol(LANES), seq(LANES), seq(WIDTH),
                  pl.BlockSpec((1, s // c_w, WIDTH, c_w), lambda b, i: (b, 0, 0, 0))],
        out_specs=pl.BlockSpec((qt, WIDTH), lambda b, i: (b * nq + i, 0)),
        out_shape=jax.ShapeDtypeStruct((bn * s, WIDTH), BF16),
        scratch_shapes=[pltpu.VMEM((s // c_w, c_w, qt), F32), pltpu.VMEM((SUBLANES, qt), I32)]
        + _attn_scratch(c_w, qt),
        compiler_params=_params("parallel", "arbitrary"),
        name="dsa_prompt",
    )(qat, qit, kwt, kwb, kab, vat4)


def _moba_prompt_kernel(qbt_ref, km_ref, kb_ref, vt_ref, o_ref, marks_ref, lg_ref, p_ref, m_ref, l_ref, acc_ref):
    cur = pl.program_id(1)
    nb, qt = marks_ref.shape[1:]
    km = km_ref[0].astype(BF16)
    past = lax.broadcasted_iota(I32, (nb, qt), 0) < cur
    for h in range(N_HEADS):
        marks_ref[h] = _top_blocks(_dot(km[:, _head(h)], qbt_ref[0, _head(h), :]), past, MOBA_TOPK, 0)
    _init_softmax_state(m_ref, l_ref, acc_ref)

    def block(j, bias):
        kc = kb_ref[pl.ds(pl.multiple_of(j * MOBA_BLOCK, MOBA_BLOCK), MOBA_BLOCK), :]
        _attend_chunk_t(kc, qbt_ref, lambda h: vt_ref[0, j, _head(h), :], bias,
                        lg_ref, p_ref, m_ref, l_ref, acc_ref)

    def blk_body(j, carry):
        block(j, lambda h: jnp.where(marks_ref[h, pl.ds(j, 1), :] > 0.5, 0.0, NEG))
        return carry

    lax.fori_loop(0, cur, blk_body, 0)
    causal = (lax.broadcasted_iota(I32, (MOBA_BLOCK, qt), 0) <= lax.broadcasted_iota(I32, (MOBA_BLOCK, qt), 1))
    own_bias = jnp.where(causal, 0.0, NEG)
    block(cur, lambda h: own_bias)
    _finish_heads_t(o_ref, l_ref, acc_ref)


def _moba_prompt_call(qbt, kmean, kbb, vbt4, bn, s):
    qt = MOBA_BLOCK
    nq = s // qt
    nb = kmean.shape[1]
    return pl.pallas_call(
        _moba_prompt_kernel,
        grid=(bn, nq),
        in_specs=[pl.BlockSpec((1, WIDTH, qt), lambda b, i: (b, 0, i)),
                  pl.BlockSpec((1, nb, WIDTH), lambda b, i: (b, 0, 0)),
                  pl.BlockSpec((s, WIDTH), lambda b, i: (b, 0)),
                  pl.BlockSpec((1, nq, WIDTH, qt), lambda b, i: (b, 0, 0, 0))],
        out_specs=pl.BlockSpec((qt, WIDTH), lambda b, i: (b * nq + i, 0)),
        out_shape=jax.ShapeDtypeStruct((bn * s, WIDTH), BF16),
        scratch_shapes=[pltpu.VMEM((N_HEADS, nb, qt), F32)] + _attn_scratch(MOBA_BLOCK, qt),
        compiler_params=_params("parallel", "arbitrary"),
        name="moba_prompt",
    )(qbt, kmean, kbb, vbt4)


def _sample_a_kernel(pt_ref, qi_ref, wcol_ref, qda_ref, qdb_ref, *refs, n_pages, nq, k_sel, idx_bits):
    del pt_ref
    pp = PAGES_PER_STEP
    idx_refs, ka_refs, kb_refs = refs[0:pp], refs[pp:2 * pp], refs[2 * pp:3 * pp]
    kin_ref, kan_ref, kbn_ref, pa_ref, pb_ref, sc_ref, y_ref, la_ref, lb_ref = refs[3 * pp:]
    p = pl.program_id(1)
    nch = n_pages + 1
    rows = N_HEADS * nq
    lane = lax.broadcasted_iota(I32, (rows, LANES), 1)
    qrow = lax.broadcasted_iota(I32, (rows, LANES), 0) % nq

    def idx_scores(k_t):
        s = jnp.maximum(_dot(qi_ref[0], k_t), 0.0) * wcol_ref[0]
        sc = s[0:nq]
        for h in range(1, N_IDX_HEADS):
            sc = sc + s[h * nq:(h + 1) * nq]
        return sc

    for t in range(pp):
        g = p * pp + t
        sc_ref[g] = idx_scores(idx_refs[t][0].astype(BF16))
        la_ref[g] = _dot(qda_ref[0], ka_refs[t][0].astype(BF16))
        lb_ref[g] = _dot(qdb_ref[0], kb_refs[t][0].astype(BF16))

    def softmax_out(l_ref, p_ref, keep):
        lg = jnp.where(keep, l_ref[...].reshape(nch, N_HEADS, nq, LANES), NEG)
        m = jnp.max(jnp.max(lg, axis=0), axis=-1, keepdims=True)
        e = jnp.exp(lg - m)
        inv = 1.0 / jnp.sum(jnp.sum(e, axis=0), axis=-1, keepdims=True)
        p_ref[0] = (e * inv).reshape(nch, rows, LANES).astype(p_ref.dtype)

    @pl.when(p == n_pages // pp - 1)
    def _():
        new_ok = (lane <= qrow) & (lane < nq)
        kidx = (lax.broadcasted_iota(I32, (nch, nq, LANES), 0) * LANES
                + lax.broadcasted_iota(I32, (nch, nq, LANES), 2))
        sc_ref[n_pages] = jnp.where(new_ok[:nq], idx_scores(kin_ref[0]), NEG)
        sc = sc_ref[...]

        def count(pred):
            return jnp.sum(jnp.sum(jnp.where(pred(sc, kidx), 1.0, 0.0), axis=0), axis=-1, keepdims=True)

        thr = _select_threshold(count, (nq, LANES), lambda v: v, k_sel, idx_bits, y_ref)
        sel = (sc > thr) | ((sc == thr) & (kidx <= y_ref[...]))
        la_ref[n_pages] = jnp.where(new_ok, _dot(qda_ref[0], kan_ref[0]), NEG)
        softmax_out(la_ref, pa_ref, sel[:, None])
        cpb = MOBA_BLOCK // PAGE_SIZE
        n_blk = n_pages // cpb
        lb_ref[n_pages] = jnp.where(new_ok, _dot(qdb_ref[0], kbn_ref[0]), NEG)
        lb_past = lb_ref[0:n_pages].reshape(n_blk, cpb, N_HEADS, nq, LANES)
        gate = jnp.sum(jnp.sum(lb_past, axis=1), axis=-1, keepdims=True) * (1.0 / MOBA_BLOCK)
        marks = _top_blocks(gate, jnp.full(gate.shape, True), MOBA_TOPK, 0)
        keep_past = jnp.broadcast_to(marks[:, None], (n_blk, cpb, N_HEADS, nq, 1)).reshape(n_pages, N_HEADS, nq, 1)
        keep = jnp.concatenate([keep_past, jnp.ones((1, N_HEADS, nq, 1), F32)], axis=0) > 0.5
        softmax_out(lb_ref, pb_ref, keep)


def _page_specs(rows):
    pp = PAGES_PER_STEP
    return [pl.BlockSpec((1, rows, PAGE_SIZE), functools.partial(
        lambda b, p, pt, t: (pt[b, p * pp + t], 0, 0), t=t)) for t in range(pp)]


def _sample_a_call(page_table, qi_rows, wcol, qda, qdb, c_idx, c_ka, c_kb, ki_new, ka_new, kb_new, nq):
    bn, n_pages = page_table.shape
    pp = PAGES_PER_STEP
    assert n_pages % pp == 0 and MOBA_BLOCK % PAGE_SIZE == 0
    rows = N_HEADS * nq
    nch = n_pages + 1
    k_sel = min(TOPK_KEYS, (n_pages * PAGE_SIZE + nq) // 4)
    idx_bits = int(nch * LANES).bit_length()
    per_b = lambda shp: pl.BlockSpec((1,) + shp, lambda b, p, pt: (b,) + (0,) * len(shp))
    kern = functools.partial(_sample_a_kernel, n_pages=n_pages, nq=nq, k_sel=k_sel, idx_bits=idx_bits)
    grid_spec = pltpu.PrefetchScalarGridSpec(
        num_scalar_prefetch=1,
        grid=(bn, n_pages // pp),
        in_specs=[per_b((rows, IDX_DIM)), per_b((rows, LANES)), per_b((rows, WIDTH)), per_b((rows, WIDTH))]
        + _page_specs(IDX_DIM) + _page_specs(WIDTH) + _page_specs(WIDTH)
        + [per_b((IDX_DIM, LANES)), per_b((WIDTH, LANES)), per_b((WIDTH, LANES))],
        out_specs=[per_b((nch, rows, LANES)), per_b((nch, rows, LANES))],
        scratch_shapes=[
            pltpu.VMEM((nch, nq, LANES), F32),
            pltpu.VMEM((nq, LANES), I32),
            pltpu.VMEM((nch, rows, LANES), F32),
            pltpu.VMEM((nch, rows, LANES), F32),
        ],
    )
    return pl.pallas_call(
        kern,
        grid_spec=grid_spec,
        out_shape=[jax.ShapeDtypeStruct((bn, nch, rows, LANES), BF16)] * 2,
        compiler_params=_params("parallel", "arbitrary"),
        name="sample_a",
    )(page_table, qi_rows, wcol, qda, qdb, *([c_idx] * pp), *([c_ka] * pp), *([c_kb] * pp),
      ki_new, ka_new, kb_new)


def _sample_b_kernel(pt_ref, pa_ref, pb_ref, pan_ref, pbn_ref, *refs, n_pages, nq):
    del pt_ref
    pp = PAGES_PER_STEP
    va_refs, vb_refs = refs[0:pp], refs[pp:2 * pp]
    van_ref, vbn_ref, oa_ref, ob_ref, acca_ref, accb_ref = refs[2 * pp:]
    p = pl.program_id(1)

    @pl.when(p == 0)
    def _():
        acca_ref[...] = _dot_nt(pan_ref[0, 0], van_ref[0])
        accb_ref[...] = _dot_nt(pbn_ref[0, 0], vbn_ref[0])

    for t in range(pp):
        acca_ref[...] += _dot_nt(pa_ref[0, t], va_refs[t][0].astype(BF16))
        accb_ref[...] += _dot_nt(pb_ref[0, t], vb_refs[t][0].astype(BF16))

    @pl.when(p == n_pages // pp - 1)
    def _():
        head_of_lane = lax.broadcasted_iota(I32, (nq, WIDTH), 1) // HEAD_DIM
        for acc_ref, o_ref in ((acca_ref, oa_ref), (accb_ref, ob_ref)):
            out = jnp.zeros((nq, WIDTH), F32)
            for h in range(N_HEADS):
                out = out + jnp.where(head_of_lane == h, acc_ref[h * nq:(h + 1) * nq, :], 0.0)
            o_ref[0] = out.astype(o_ref.dtype)


def _sample_b_call(page_table, pa, pb, c_va, c_vb, va_new, vb_new, nq):
    bn, n_pages = page_table.shape
    pp = PAGES_PER_STEP
    rows = N_HEADS * nq
    per_b = lambda shp: pl.BlockSpec((1,) + shp, lambda b, p, pt: (b,) + (0,) * len(shp))
    pcol = pl.BlockSpec((1, pp, rows, LANES), lambda b, p, pt: (b, p, 0, 0))
    pnew = pl.BlockSpec((1, 1, rows, LANES), lambda b, p, pt: (b, n_pages, 0, 0))
    kern = functools.partial(_sample_b_kernel, n_pages=n_pages, nq=nq)
    grid_spec = pltpu.PrefetchScalarGridSpec(
        num_scalar_prefetch=1,
        grid=(bn, n_pages // pp),
        in_specs=[pcol, pcol, pnew, pnew] + _page_specs(WIDTH) + _page_specs(WIDTH)
        + [per_b((WIDTH, LANES)), per_b((WIDTH, LANES))],
        out_specs=[per_b((nq, WIDTH)), per_b((nq, WIDTH))],
        scratch_shapes=[pltpu.VMEM((rows, WIDTH), F32), pltpu.VMEM((rows, WIDTH), F32)],
    )
    return pl.pallas_call(
        kern,
        grid_spec=grid_spec,
        out_shape=[jax.ShapeDtypeStruct((bn, nq, WIDTH), BF16)] * 2,
        compiler_params=_params("parallel", "arbitrary"),
        name="sample_b",
    )(page_table, pa, pb, pa, pb, *([c_va] * pp), *([c_vb] * pp), va_new, vb_new)


def _merge_ffn_kernel(x_ref, ya_ref, yb_ref, ga_ref, gb_ref, wa_ref, wb_ref, wo_ref, gf_ref,
                      wg_ref, wu_ref, wd_ref, gfin_ref, y_ref, *, ff_slab):
    za = _dot(ya_ref[...], wa_ref[...])
    zb = _dot(yb_ref[...], wb_ref[...])
    mix = jax.nn.sigmoid(ga_ref[...]) * za + jax.nn.sigmoid(gb_ref[...]) * zb
    x1 = x_ref[...] + _dot(mix.astype(BF16), wo_ref[...])
    ms = jnp.mean(x1 * x1, axis=-1, keepdims=True)
    xn = (x1 * lax.rsqrt(ms + RMS_EPS) * gf_ref[...]).astype(BF16)
    x2 = x1
    d_ff = wg_ref.shape[1]
    for s0 in range(0, d_ff, ff_slab):
        hg = _dot(xn, wg_ref[:, s0:s0 + ff_slab])
        hu = _dot(xn, wu_ref[:, s0:s0 + ff_slab])
        act = (hg * jax.nn.sigmoid(hg) * hu).astype(BF16)
        x2 = x2 + _dot(act, wd_ref[s0:s0 + ff_slab, :])
    ms2 = jnp.mean(x2 * x2, axis=-1, keepdims=True)
    y_ref[...] = x2 * lax.rsqrt(ms2 + RMS_EPS) * gfin_ref[...]


def _merge_ffn_call(x, ya, yb, ga, gb, wa, wb, wo, gf, wg, wu, wd, gfin, tm):
    m, d = x.shape
    d_ff = wg.shape[1]
    ff_slab = d_ff // 2 if (d_ff // 2) % LANES == 0 else d_ff
    row = lambda w: pl.BlockSpec((tm, w), lambda i: (i, 0))
    kern = functools.partial(_merge_ffn_kernel, ff_slab=ff_slab)
    return pl.pallas_call(
        kern,
        grid=(m // tm,),
        in_specs=[row(d), row(WIDTH), row(WIDTH), row(d), row(d),
                  _const_spec(wa.shape), _const_spec(wb.shape), _const_spec(wo.shape), _const_spec((1, d)),
                  _const_spec(wg.shape), _const_spec(wu.shape), _const_spec(wd.shape), _const_spec((1, d))],
        out_specs=row(d),
        out_shape=jax.ShapeDtypeStruct((m, d), F32),
        compiler_params=_params("parallel"),
        name="merge_ffn",
    )(x, ya, yb, ga, gb, wa, wb, wo, gf.reshape(1, d), wg, wu, wd, gfin.reshape(1, d))


def _layer(x_p, x_s, cache_ka, cache_va, cache_idx_k, cache_kb, cache_vb, page_table,
           g_attn, w_in, g_idx_k, w_proj_a, w_proj_b, w_out, g_ffn, w_gate, w_up, w_down, g_final):
    bn, s, d = x_p.shape
    db, nq, _ = x_s.shape
    n_pages = page_table.shape[1]
    n_pool = cache_ka.shape[0]
    past = n_pages * PAGE_SIZE
    assert s % PROJ_ROWS == 0 and Q_TILE == MOBA_BLOCK and PROJ_ROWS % MOBA_BLOCK == 0

    w_arr = _arrange_w_in(w_in)
    gk = jnp.pad(g_idx_k, (0, LANES - IDX_DIM)).reshape(1, LANES)
    wa, wb, wo = w_proj_a.astype(BF16), w_proj_b.astype(BF16), w_out.astype(BF16)
    wg, wu, wd = w_gate.astype(BF16), w_up.astype(BF16), w_down.astype(BF16)

    cs_p, sn_p = _rope_tables(jnp.arange(s))
    (qat, qit, qbt, kab, kbb, vat4, vbt4, kat, vat, kbt, vbt, ga, gb, kwb, kwt, kmean) = _proj_prompt_call(
        x_p.reshape(bn * s, d), g_attn, w_arr, cs_p, sn_p, gk, bn, s)
    ya = _dsa_prompt_call(qat, qit, kwt, kwb, kab, vat4, bn, s)
    nb = s // MOBA_BLOCK
    kmean_b = jnp.pad(kmean.reshape(bn, nb, WIDTH), ((0, 0), (0, -nb % SUBLANES), (0, 0)))
    yb = _moba_prompt_call(qbt, kmean_b, kbb, vbt4, bn, s)
    y_p = _merge_ffn_call(x_p.reshape(bn * s, d), ya, yb, ga, gb, wa, wb, wo, g_ffn, wg, wu, wd, g_final,
                          FFN_ROWS)
    heads_out = lambda t: t.reshape(bn, N_HEADS, HEAD_DIM, s).transpose(0, 3, 1, 2)
    new_p = (heads_out(kat), heads_out(vat), kwt[:, :IDX_DIM, :].transpose(0, 2, 1), heads_out(kbt), heads_out(vbt))

    ms_ = db * nq
    cs_s, sn_s = _rope_tables(past + jnp.arange(nq))
    cs_s, sn_s = jnp.tile(cs_s, (db, 1)), jnp.tile(sn_s, (db, 1))
    (qa_s, qi_s, qb_s, kaf_s, vaf_s, kbf_s, vbf_s, kab_s, vab_s, kbb_s, vbb_s, ga_s, gb_s, kwf_s, kwb_s) = \
        _proj_sample_call(x_s.reshape(ms_, d), g_attn, w_arr, cs_s, sn_s, gk)
    heads = lambda t: t.reshape(db, nq, N_HEADS, HEAD_DIM).transpose(0, 2, 1, 3)
    eye = jnp.eye(N_HEADS, dtype=BF16)
    blockdiag = lambda t: jnp.einsum("bhqd,hg->bhqgd", heads(t), eye).reshape(db, N_HEADS * nq, WIDTH)
    qi_rows = heads(qi_s).reshape(db, N_HEADS * nq, IDX_DIM)
    wi_s = kwf_s[:, IDX_DIM:IDX_DIM + N_IDX_HEADS].reshape(db, nq, N_IDX_HEADS).transpose(0, 2, 1)
    wcol = jnp.broadcast_to(wi_s.reshape(db, N_IDX_HEADS * nq, 1), (db, N_IDX_HEADS * nq, LANES))
    new_t = lambda t: jnp.pad(t.reshape(db, nq, t.shape[-1]).transpose(0, 2, 1), ((0, 0), (0, 0), (0, LANES - nq)))
    page_t = lambda c: c.transpose(0, 2, 3, 1).reshape(n_pool, WIDTH, PAGE_SIZE)
    pa, pb = _sample_a_call(
        page_table, qi_rows, wcol, blockdiag(qa_s), blockdiag(qb_s),
        cache_idx_k.transpose(0, 2, 1), page_t(cache_ka), page_t(cache_kb),
        new_t(kwb_s[:, :IDX_DIM]), new_t(kab_s), new_t(kbb_s), nq)
    ya_s, yb_s = _sample_b_call(page_table, pa, pb, page_t(cache_va), page_t(cache_vb),
                                new_t(vab_s), new_t(vbb_s), nq)
    y_s = _merge_ffn_call(x_s.reshape(ms_, d), ya_s.reshape(ms_, WIDTH), yb_s.reshape(ms_, WIDTH), ga_s, gb_s,
                          wa, wb, wo, g_ffn, wg, wu, wd, g_final, ms_)
    new_s = (kaf_s.reshape(db, nq, N_HEADS, HEAD_DIM), vaf_s.reshape(db, nq, N_HEADS, HEAD_DIM),
             kwf_s[:, :IDX_DIM].reshape(db, nq, IDX_DIM),
             kbf_s.reshape(db, nq, N_HEADS, HEAD_DIM), vbf_s.reshape(db, nq, N_HEADS, HEAD_DIM))
    return y_p.reshape(bn, s, d), y_s.reshape(db, nq, d), new_p, new_s


def kernel(x_prompt, x_sample, cache_ka, cache_va, cache_idx_k, cache_kb, cache_vb, page_table,
           g_attn, w_in, g_idx_k, w_proj_a, w_proj_b, w_out, g_ffn, w_gate, w_up, w_down, g_final):
    assert w_in.shape[0] == 1, "the fused final norm assumes a single layer"
    y_p, y_s, new_p, new_s = _layer(
        x_prompt, x_sample, cache_ka[0], cache_va[0], cache_idx_k[0], cache_kb[0], cache_vb[0], page_table,
        g_attn[0], w_in[0], g_idx_k[0], w_proj_a[0], w_proj_b[0], w_out[0], g_ffn[0],
        w_gate[0], w_up[0], w_down[0], g_final)
    return (y_p, y_s) + tuple(t[None] for t in new_p) + tuple(t[None] for t in new_s)
```

```python
import functools

import jax
import jax.numpy as jnp
from jax import lax
from jax.experimental import pallas as pl
from jax.experimental.pallas import tpu as pltpu

F32 = jnp.float32
BF16 = jnp.bfloat16
I32 = jnp.int32

HEAD_DIM = 64
N_HEADS = 8
WIDTH = N_HEADS * HEAD_DIM
IDX_DIM = 64
N_IDX_HEADS = 8
TOPK_KEYS = 256
MOBA_BLOCK = 256
MOBA_TOPK = 3
PAGE_SIZE = 128
ROPE_THETA = 10000.0
RMS_EPS = 1e-6
NEG = -1e30
LOG2_E = 1.4426950408889634
LANES = 128
SUBLANES = 8
INT_MIN = -(2 ** 31)
VMEM_LIMIT = 56 * 1024 * 1024
Q_TILE = 256
DSA_CHUNK = 256
PROJ_ROWS = 512
FFN_ROWS = 256
PAGES_PER_STEP = 8

_NT = (((1,), (1,)), ((), ()))


def _dot_nt(a, b):
    return lax.dot_general(a, b, _NT, preferred_element_type=F32)


def _dot(a, b):
    return jnp.dot(a, b, preferred_element_type=F32)


def _head(h):
    return slice(h * HEAD_DIM, (h + 1) * HEAD_DIM)


def _params(*sem):
    return pltpu.CompilerParams(dimension_semantics=sem, vmem_limit_bytes=VMEM_LIMIT)


def _const_spec(shape):
    nd = len(shape)
    return pl.BlockSpec(shape, lambda *_: (0,) * nd, pipeline_mode=pl.Buffered(1))


def _rope_tables(pos):
    inv_freq = ROPE_THETA ** (-jnp.arange(0, HEAD_DIM, 2, dtype=F32) / HEAD_DIM)
    ang = pos.astype(F32)[:, None] * inv_freq[None, :]
    cos, sin = jnp.cos(ang), jnp.sin(ang)
    return (jnp.concatenate([cos, cos, cos, cos], axis=-1),
            jnp.concatenate([-sin, sin, -sin, sin], axis=-1))


def _proj_kernel(x_ref, g_ref, w_ref, cs_ref, sn_ref, gk_ref, *outs, transposed, va_tile, vb_tile):
    tm = x_ref.shape[0]
    x = x_ref[...]
    ms = jnp.mean(x * x, axis=-1, keepdims=True)
    xn = (x * lax.rsqrt(ms + RMS_EPS) * g_ref[...]).astype(BF16)
    cs = cs_ref[...]
    sn = sn_ref[...]
    lane = lax.broadcasted_iota(I32, (tm, LANES), 1)
    first_half = (lane & (HEAD_DIM // 2)) == 0

    def rope(p):
        swapped = jnp.where(first_half, pltpu.roll(p, LANES - HEAD_DIM // 2, 1),
                            pltpu.roll(p, HEAD_DIM // 2, 1))
        return p * cs + swapped * sn

    def slab(j, width=WIDTH):
        return _dot(xn, w_ref[:, j * WIDTH:j * WIDTH + width])

    def rope_slab(h):
        return jnp.concatenate(
            [rope(h[:, k * LANES:(k + 1) * LANES]) for k in range(WIDTH // LANES)], axis=1)

    qa, qi, qb = rope_slab(slab(0)), rope_slab(slab(1)), rope_slab(slab(2))
    ka, va, kb, vb = rope_slab(slab(3)), slab(4), rope_slab(slab(5)), slab(6)
    t = slab(11, LANES)
    is_key = lane < IDX_DIM
    kms = jnp.sum(jnp.where(is_key, t * t, 0.0), axis=-1, keepdims=True) * (1.0 / IDX_DIM)
    kn = rope(t * lax.rsqrt(kms + RMS_EPS) * gk_ref[...])
    kw = jnp.where(is_key, kn, jnp.where(lane < IDX_DIM + N_IDX_HEADS, t * (N_IDX_HEADS ** -0.5), 0.0))

    if transposed:
        (qat_ref, qit_ref, qbt_ref, kab_ref, kbb_ref, vat4_ref, vbt4_ref,
         kat_ref, vat_ref, kbt_ref, vbt_ref, ga_ref, gb_ref, kwb_ref, kwt_ref, kmean_ref) = outs
        qat_ref[0] = qa.T.astype(BF16)
        qit_ref[0] = qi.T.astype(BF16)
        qbt_ref[0] = qb.T.astype(BF16)
        kab_ref[...] = ka.astype(BF16)
        kbb_ref[...] = kb.astype(BF16)
        kat_ref[0] = ka.T
        kbt_ref[0] = kb.T
        va_t, vb_t = va.T, vb.T
        vat_ref[0] = va_t
        vbt_ref[0] = vb_t
        for r in range(tm // va_tile):
            vat4_ref[0, r] = va_t[:, r * va_tile:(r + 1) * va_tile].astype(BF16)
        for r in range(tm // vb_tile):
            vbt4_ref[0, r] = vb_t[:, r * vb_tile:(r + 1) * vb_tile].astype(BF16)
        kwb_ref[...] = kw.astype(BF16)
        kwt_ref[0] = kw.T
        for r in range(tm // MOBA_BLOCK):
            kmean_ref[0, r:r + 1, :] = jnp.mean(kb[r * MOBA_BLOCK:(r + 1) * MOBA_BLOCK], axis=0, keepdims=True)
    else:
        (qa_ref, qi_ref, qb_ref, kaf_ref, vaf_ref, kbf_ref, vbf_ref,
         kab_ref, vab_ref, kbb_ref, vbb_ref, ga_ref, gb_ref, kwf_ref, kwb_ref) = outs
        qa_ref[...] = qa.astype(BF16)
        qi_ref[...] = qi.astype(BF16)
        qb_ref[...] = qb.astype(BF16)
        for f_ref, b_ref, val in ((kaf_ref, kab_ref, ka), (vaf_ref, vab_ref, va),
                                  (kbf_ref, kbb_ref, kb), (vbf_ref, vbb_ref, vb)):
            f_ref[...] = val
            b_ref[...] = val.astype(BF16)
        kwf_ref[...] = kw
        kwb_ref[...] = kw.astype(BF16)
    ga_ref[:, :WIDTH] = slab(7)
    ga_ref[:, WIDTH:] = slab(8)
    gb_ref[:, :WIDTH] = slab(9)
    gb_ref[:, WIDTH:] = slab(10)


def _arrange_w_in(w_in):
    d = w_in.shape[0]
    o = [0]
    for s in (WIDTH, WIDTH, WIDTH, N_IDX_HEADS * IDX_DIM, IDX_DIM, N_IDX_HEADS, WIDTH, WIDTH, WIDTH, d, d):
        o.append(o[-1] + s)
    qa, ka, va, qi, ki, wi, qb, kb, vb, ga, gb = [w_in[:, o[i]:o[i + 1]] for i in range(11)]
    qscale = HEAD_DIM ** -0.5 * LOG2_E
    iscale = IDX_DIM ** -0.5
    pad = jnp.zeros((d, LANES - IDX_DIM - N_IDX_HEADS), w_in.dtype)
    return jnp.concatenate([qa * qscale, qi * iscale, qb * qscale, ka, va, kb, vb, ga, gb, ki, wi, pad],
                           axis=1).astype(BF16)


def _proj_prompt_call(x, g_attn, w_arr, cs, sn, gk, bn, s):
    m, d = x.shape
    tm = PROJ_ROWS
    spt = s // tm
    row = lambda w: pl.BlockSpec((tm, w), lambda i: (i, 0))
    tab = pl.BlockSpec((tm, LANES), lambda i: (i % spt, 0))
    col = lambda r: pl.BlockSpec((1, r, tm), lambda i: (i // spt, 0, i % spt))
    tile4 = lambda w: pl.BlockSpec((1, tm // w, WIDTH, w), lambda i: (i // spt, i % spt, 0, 0))
    t_shape = lambda r, dt: jax.ShapeDtypeStruct((bn, r, s), dt)
    out_shape = (
        [t_shape(WIDTH, BF16)] * 3
        + [jax.ShapeDtypeStruct((m, WIDTH), BF16)] * 2
        + [jax.ShapeDtypeStruct((bn, s // DSA_CHUNK, WIDTH, DSA_CHUNK), BF16),
           jax.ShapeDtypeStruct((bn, s // MOBA_BLOCK, WIDTH, MOBA_BLOCK), BF16)]
        + [t_shape(WIDTH, F32)] * 4
        + [jax.ShapeDtypeStruct((m, d), F32)] * 2
        + [jax.ShapeDtypeStruct((m, LANES), BF16), t_shape(LANES, F32),
           jax.ShapeDtypeStruct((m // tm, tm // MOBA_BLOCK, WIDTH), F32)])
    out_specs = ([col(WIDTH)] * 3 + [row(WIDTH)] * 2 + [tile4(DSA_CHUNK), tile4(MOBA_BLOCK)]
                 + [col(WIDTH)] * 4 + [row(d)] * 2 + [row(LANES), col(LANES),
                 pl.BlockSpec((1, tm // MOBA_BLOCK, WIDTH), lambda i: (i, 0, 0))])
    kern = functools.partial(_proj_kernel, transposed=True, va_tile=DSA_CHUNK, vb_tile=MOBA_BLOCK)
    return pl.pallas_call(
        kern,
        grid=(m // tm,),
        in_specs=[row(d), _const_spec((1, d)), _const_spec(w_arr.shape), tab, tab, _const_spec((1, LANES))],
        out_specs=out_specs,
        out_shape=out_shape,
        compiler_params=_params("parallel"),
        name="proj",
    )(x, g_attn.reshape(1, d), w_arr, cs, sn, gk)


def _proj_sample_call(x, g_attn, w_arr, cs, sn, gk):
    m, d = x.shape
    row = lambda w: pl.BlockSpec((m, w), lambda i: (0, 0))
    out_shape = ([jax.ShapeDtypeStruct((m, WIDTH), BF16)] * 3 + [jax.ShapeDtypeStruct((m, WIDTH), F32)] * 4
                 + [jax.ShapeDtypeStruct((m, WIDTH), BF16)] * 4 + [jax.ShapeDtypeStruct((m, d), F32)] * 2
                 + [jax.ShapeDtypeStruct((m, LANES), F32), jax.ShapeDtypeStruct((m, LANES), BF16)])
    kern = functools.partial(_proj_kernel, transposed=False, va_tile=0, vb_tile=0)
    return pl.pallas_call(
        kern,
        grid=(1,),
        in_specs=[row(d), _const_spec((1, d)), _const_spec(w_arr.shape), row(LANES), row(LANES),
                  _const_spec((1, LANES))],
        out_specs=[row(WIDTH)] * 11 + [row(d)] * 2 + [row(LANES)] * 2,
        out_shape=out_shape,
        compiler_params=_params("arbitrary"),
        name="proj_sample",
    )(x, g_attn.reshape(1, d), w_arr, cs, sn, gk)


def _key_to_float(key):
    return lax.bitcast_convert_type(key ^ ((key >> 31) & jnp.int32(0x7FFFFFFF)), F32)


def _select_threshold(count, state_shape, bc, k_sel, idx_bits, y_ref, unroll=False):
    kf = float(k_sel)
    rep = lambda v: jnp.broadcast_to(v, state_shape)

    def bis(t, key):
        cand = key + lax.shift_left(jnp.int32(1), 31 - t)
        cf = bc(_key_to_float(cand))
        return jnp.where(rep(count(lambda x, _: x >= cf)) >= kf, cand, key)

    thr = _key_to_float(lax.fori_loop(0, 32, bis, jnp.full(state_shape, INT_MIN, I32), unroll=unroll))
    tb = bc(thr)
    c_ge = count(lambda x, _: x >= tb)
    y_ref[...] = jnp.full(state_shape, 2 ** 30, I32)

    @pl.when(jnp.max(c_ge) > kf)
    def _():
        need = rep(kf - count(lambda x, _: x > tb))

        def ybis(t, y):
            cand = y + lax.shift_left(jnp.int32(1), idx_bits - 1 - t)
            cb = bc(cand)
            f = count(lambda x, idx: (x == tb) & (idx < cb))
            return jnp.where(rep(f) < need, cand, y)

        y_ref[...] = lax.fori_loop(0, idx_bits, ybis, jnp.zeros(state_shape, I32))

    return thr


def _top_blocks(gate, valid, n_top, axis):
    n = gate.shape[axis]
    ids = lax.broadcasted_iota(I32, gate.shape, axis).astype(F32)
    g = jnp.where(valid, gate, NEG)
    marks = jnp.zeros(gate.shape, F32)
    for _ in range(n_top):
        mx = jnp.max(g, axis=axis, keepdims=True)
        first = jnp.min(jnp.where(g == mx, ids, float(n)), axis=axis, keepdims=True)
        hit = ids == first
        marks = jnp.where(hit, 1.0, marks)
        g = jnp.where(hit, -jnp.inf, g)
    return jnp.where(valid, marks, 0.0)


def _attend_chunk_t(kc, qt_ref, vt, bias, lg_ref, p_ref, m_ref, l_ref, acc_ref):
    for h in range(N_HEADS):
        lg_ref[h] = _dot(kc[:, _head(h)], qt_ref[0, _head(h), :]) + bias(h)
    for h in range(N_HEADS):
        lg = lg_ref[h]
        m_old = m_ref[h]
        m_new = jnp.maximum(m_old, jnp.max(lg, axis=0, keepdims=True))
        alpha = jnp.exp2(m_old - m_new)
        p = jnp.exp2(lg - m_new[0:1])
        l_ref[h] = alpha * l_ref[h] + jnp.sum(p, axis=0, keepdims=True)
        m_ref[h] = m_new
        p_ref[h] = p.astype(BF16)
        acc_ref[_head(h), :] = alpha[0:1] * acc_ref[_head(h), :]
    for h in range(N_HEADS):
        acc_ref[_head(h), :] += _dot(vt(h), p_ref[h])


def _init_softmax_state(m_ref, l_ref, acc_ref):
    m_ref[...] = jnp.full(m_ref.shape, NEG, F32)
    l_ref[...] = jnp.zeros(l_ref.shape, F32)
    acc_ref[...] = jnp.zeros(acc_ref.shape, F32)


def _finish_heads_t(o_ref, l_ref, acc_ref):
    for h in range(N_HEADS):
        acc_ref[_head(h), :] = acc_ref[_head(h), :] / l_ref[h][0:1]
    o_ref[...] = acc_ref[...].T.astype(o_ref.dtype)


def _attn_scratch(c_w, qt):
    return [pltpu.VMEM((N_HEADS, c_w, qt), F32), pltpu.VMEM((N_HEADS, c_w, qt), BF16),
            pltpu.VMEM((N_HEADS, SUBLANES, qt), F32), pltpu.VMEM((N_HEADS, SUBLANES, qt), F32),
            pltpu.VMEM((WIDTH, qt), F32)]


def _dsa_prompt_kernel(qat_ref, qit_ref, kwt_ref, kw_ref, ka_ref, vt_ref, o_ref,
                       sc_ref, y_ref, lg_ref, p_ref, m_ref, l_ref, acc_ref, *, k_sel, idx_bits):
    i = pl.program_id(1)
    _, c_w, qt = sc_ref.shape
    nkc = ((i + 1) * qt + c_w - 1) // c_w
    row = lax.broadcasted_iota(I32, (c_w, qt), 0)
    qpos = i * qt + lax.broadcasted_iota(I32, (c_w, qt), 1)

    def score_body(c, carry):
        k0 = pl.multiple_of(c * c_w, c_w)
        ki = kw_ref[pl.ds(k0, c_w), :][:, :IDX_DIM]
        acc = jnp.zeros((c_w, qt), F32)
        for h in range(N_IDX_HEADS):
            s = _dot(ki, qit_ref[0, h * IDX_DIM:(h + 1) * IDX_DIM, :])
            acc = acc + jnp.maximum(s, 0.0) * kwt_ref[0, IDX_DIM + h:IDX_DIM + h + 1, :]
        sc_ref[c] = jnp.where((k0 + row) <= qpos, acc, NEG)
        return carry

    lax.fori_loop(0, nkc, score_body, 0)

    @pl.when(nkc % 2 == 1)
    def _():
        sc_ref[nkc] = jnp.full((c_w, qt), NEG, F32)

    def count(pred):
        def body(c2, cnt):
            for c in (2 * c2, 2 * c2 + 1):
                hit = jnp.where(pred(sc_ref[c], c * c_w + row), 1.0, 0.0)
                cnt = cnt + jnp.sum(hit.reshape(c_w // SUBLANES, SUBLANES, qt), axis=0)
            return cnt

        cnt = lax.fori_loop(0, (nkc + 1) // 2, body, jnp.zeros((SUBLANES, qt), F32))
        return jnp.sum(cnt, axis=0, keepdims=True)

    thr = _select_threshold(count, (SUBLANES, qt), lambda v: v[0:1], k_sel, idx_bits, y_ref)
    tb = thr[0:1]
    yb = y_ref[0:1, :]
    _init_softmax_state(m_ref, l_ref, acc_ref)

    def att_body(c, carry):
        k0 = pl.multiple_of(c * c_w, c_w)
        x = sc_ref[c]
        kidx = k0 + row
        sel = ((x > tb) | ((x == tb) & (kidx <= yb))) & (kidx <= qpos)
        bias = jnp.where(sel, 0.0, NEG)
        _attend_chunk_t(ka_ref[pl.ds(k0, c_w), :], qat_ref, lambda h: vt_ref[0, c, _head(h), :],
                        lambda h: bias, lg_ref, p_ref, m_ref, l_ref, acc_ref)
        return carry

    lax.fori_loop(0, nkc, att_body, 0)
    _finish_heads_t(o_ref, l_ref, acc_ref)


def _dsa_prompt_call(qat, qit, kwt, kwb, kab, vat4, bn, s):
    qt, c_w = Q_TILE, DSA_CHUNK
    assert (s // c_w) % 2 == 0
    nq = s // qt
    k_sel = min(TOPK_KEYS, s // 4)
    idx_bits = int(s).bit_length()
    qcol = lambda r: pl.BlockSpec((1, r, qt), lambda b, i: (b, 0, i))
    seq = lambda w: pl.BlockSpec((s, w), lambda b, i: (b, 0))
    kern = functools.partial(_dsa_prompt_kernel, k_sel=k_sel, idx_bits=idx_bits)
    return pl.pallas_call(
        kern,
        grid=(bn, nq),
        in_specs=[qcol(WIDTH), qcol(WIDTH), qcol(LANES), seq(LANES), seq(WIDTH),
                  pl.BlockSpec((1, s // c_w, WIDTH, c_w), lambda b, i: (b, 0, 0, 0))],
        out_specs=pl.BlockSpec((qt, WIDTH), lambda b, i: (b * nq + i, 0)),
        out_shape=jax.ShapeDtypeStruct((bn * s, WIDTH), BF16),
        scratch_shapes=[pltpu.VMEM((s // c_w, c_w, qt), F32), pltpu.VMEM((SUBLANES, qt), I32)]
        + _attn_scratch(c_w, qt),
        compiler_params=_params("parallel", "arbitrary"),
        name="dsa_prompt",
    )(qat, qit, kwt, kwb, kab, vat4)


def _moba_prompt_kernel(qbt_ref, km_ref, kb_ref, vt_ref, o_ref, marks_ref, lg_ref, p_ref, m_ref, l_ref, acc_ref):
    cur = pl.program_id(1)
    nb, qt = marks_ref.shape[1:]
    km = km_ref[0].astype(BF16)
    past = lax.broadcasted_iota(I32, (nb, qt), 0) < cur
    for h in range(N_HEADS):
        marks_ref[h] = _top_blocks(_dot(km[:, _head(h)], qbt_ref[0, _head(h), :]), past, MOBA_TOPK, 0)
    _init_softmax_state(m_ref, l_ref, acc_ref)

    def block(j, bias):
        kc = kb_ref[pl.ds(pl.multiple_of(j * MOBA_BLOCK, MOBA_BLOCK), MOBA_BLOCK), :]
        _attend_chunk_t(kc, qbt_ref, lambda h: vt_ref[0, j, _head(h), :], bias,
                        lg_ref, p_ref, m_ref, l_ref, acc_ref)

    def blk_body(j, carry):
        block(j, lambda h: jnp.where(marks_ref[h, pl.ds(j, 1), :] > 0.5, 0.0, NEG))
        return carry

    lax.fori_loop(0, cur, blk_body, 0)
    causal = (lax.broadcasted_iota(I32, (MOBA_BLOCK, qt), 0) <= lax.broadcasted_iota(I32, (MOBA_BLOCK, qt), 1))
    own_bias = jnp.where(causal, 0.0, NEG)
    block(cur, lambda h: own_bias)
    _finish_heads_t(o_ref, l_ref, acc_ref)


def _moba_prompt_call(qbt, kmean, kbb, vbt4, bn, s):
    qt = MOBA_BLOCK
    nq = s // qt
    nb = kmean.shape[1]
    return pl.pallas_call(
        _moba_prompt_kernel,
        grid=(bn, nq),
        in_specs=[pl.BlockSpec((1, WIDTH, qt), lambda b, i: (b, 0, i)),
                  pl.BlockSpec((1, nb, WIDTH), lambda b, i: (b, 0, 0)),
                  pl.BlockSpec((s, WIDTH), lambda b, i: (b, 0)),
                  pl.BlockSpec((1, nq, WIDTH, qt), lambda b, i: (b, 0, 0, 0))],
        out_specs=pl.BlockSpec((qt, WIDTH), lambda b, i: (b * nq + i, 0)),
        out_shape=jax.ShapeDtypeStruct((bn * s, WIDTH), BF16),
        scratch_shapes=[pltpu.VMEM((N_HEADS, nb, qt), F32)] + _attn_scratch(MOBA_BLOCK, qt),
        compiler_params=_params("parallel", "arbitrary"),
        name="moba_prompt",
    )(qbt, kmean, kbb, vbt4)


def _sample_a_kernel(pt_ref, qi_ref, wcol_ref, qda_ref, qdb_ref, *refs, n_pages, nq, k_sel, idx_bits):
    del pt_ref
    pp = PAGES_PER_STEP
    idx_refs, ka_refs, kb_refs = refs[0:pp], refs[pp:2 * pp], refs[2 * pp:3 * pp]
    kin_ref, kan_ref, kbn_ref, pa_ref, pb_ref, sc_ref, y_ref, la_ref, lb_ref = refs[3 * pp:]
    p = pl.program_id(1)
    nch = n_pages + 1
    rows = N_HEADS * nq
    lane = lax.broadcasted_iota(I32, (rows, LANES), 1)
    qrow = lax.broadcasted_iota(I32, (rows, LANES), 0) % nq

    def idx_scores(k_t):
        s = jnp.maximum(_dot(qi_ref[0], k_t), 0.0) * wcol_ref[0]
        sc = s[0:nq]
        for h in range(1, N_IDX_HEADS):
            sc = sc + s[h * nq:(h + 1) * nq]
        return sc

    for t in range(pp):
        g = p * pp + t
        sc_ref[g] = idx_scores(idx_refs[t][0].astype(BF16))
        la_ref[g] = _dot(qda_ref[0], ka_refs[t][0].astype(BF16))
        lb_ref[g] = _dot(qdb_ref[0], kb_refs[t][0].astype(BF16))

    def softmax_out(l_ref, p_ref, keep):
        lg = jnp.where(keep, l_ref[...].reshape(nch, N_HEADS, nq, LANES), NEG)
        m = jnp.max(jnp.max(lg, axis=0), axis=-1, keepdims=True)
        e = jnp.exp2(lg - m)
        inv = 1.0 / jnp.sum(jnp.sum(e, axis=0), axis=-1, keepdims=True)
        p_ref[0] = (e * inv).reshape(nch, rows, LANES).astype(p_ref.dtype)

    @pl.when(p == n_pages // pp - 1)
    def _():
        new_ok = (lane <= qrow) & (lane < nq)
        kidx = (lax.broadcasted_iota(I32, (nch, nq, LANES), 0) * LANES
                + lax.broadcasted_iota(I32, (nch, nq, LANES), 2))
        cpb = MOBA_BLOCK // PAGE_SIZE
        n_blk = n_pages // cpb
        lb_ref[n_pages] = jnp.where(new_ok, _dot(qdb_ref[0], kbn_ref[0]), NEG)
        gate = jnp.zeros((rows, LANES), F32)
        for j in range(n_blk):
            tot = lb_ref[j * cpb]
            for c in range(1, cpb):
                tot = tot + lb_ref[j * cpb + c]
            gate = jnp.where(lane == j, jnp.sum(tot, axis=-1, keepdims=True) * (1.0 / MOBA_BLOCK), gate)
        marks = _top_blocks(gate, lane < n_blk, MOBA_TOPK, 1)
        cols = [jnp.sum(jnp.where(lane == j, marks, 0.0), axis=-1, keepdims=True) for j in range(n_blk)]
        keep = jnp.stack([jnp.broadcast_to(cols[c // cpb], (rows, LANES)) for c in range(n_pages)]
                         + [jnp.ones((rows, LANES), F32)], axis=0)
        softmax_out(lb_ref, pb_ref, keep.reshape(nch, N_HEADS, nq, LANES) > 0.5)
        sc_ref[n_pages] = jnp.where(new_ok[:nq], idx_scores(kin_ref[0]), NEG)
        sc = sc_ref[...]

        def count(pred):
            return jnp.sum(jnp.sum(jnp.where(pred(sc, kidx), 1.0, 0.0), axis=0), axis=-1, keepdims=True)

        thr = _select_threshold(count, (nq, LANES), lambda v: v, k_sel, idx_bits, y_ref, unroll=True)
        sel = (sc > thr) | ((sc == thr) & (kidx <= y_ref[...]))
        la_ref[n_pages] = jnp.where(new_ok, _dot(qda_ref[0], kan_ref[0]), NEG)
        softmax_out(la_ref, pa_ref, sel[:, None])


def _page_specs(rows):
    pp = PAGES_PER_STEP
    return [pl.BlockSpec((1, rows, PAGE_SIZE), functools.partial(
        lambda b, p, pt, t: (pt[b, p * pp + t], 0, 0), t=t)) for t in range(pp)]


def _sample_a_call(page_table, qi_rows, wcol, qda, qdb, c_idx, c_ka, c_kb, ki_new, ka_new, kb_new, nq):
    bn, n_pages = page_table.shape
    pp = PAGES_PER_STEP
    assert n_pages % pp == 0 and MOBA_BLOCK % PAGE_SIZE == 0
    rows = N_HEADS * nq
    nch = n_pages + 1
    k_sel = min(TOPK_KEYS, (n_pages * PAGE_SIZE + nq) // 4)
    idx_bits = int(nch * LANES).bit_length()
    per_b = lambda shp: pl.BlockSpec((1,) + shp, lambda b, p, pt: (b,) + (0,) * len(shp))
    kern = functools.partial(_sample_a_kernel, n_pages=n_pages, nq=nq, k_sel=k_sel, idx_bits=idx_bits)
    grid_spec = pltpu.PrefetchScalarGridSpec(
        num_scalar_prefetch=1,
        grid=(bn, n_pages // pp),
        in_specs=[per_b((rows, IDX_DIM)), per_b((rows, LANES)), per_b((rows, WIDTH)), per_b((rows, WIDTH))]
        + _page_specs(IDX_DIM) + _page_specs(WIDTH) + _page_specs(WIDTH)
        + [per_b((IDX_DIM, LANES)), per_b((WIDTH, LANES)), per_b((WIDTH, LANES))],
        out_specs=[per_b((nch, rows, LANES)), per_b((nch, rows, LANES))],
        scratch_shapes=[
            pltpu.VMEM((nch, nq, LANES), F32),
            pltpu.VMEM((nq, LANES), I32),
            pltpu.VMEM((nch, rows, LANES), F32),
            pltpu.VMEM((nch, rows, LANES), F32),
        ],
    )
    return pl.pallas_call(
        kern,
        grid_spec=grid_spec,
        out_shape=[jax.ShapeDtypeStruct((bn, nch, rows, LANES), BF16)] * 2,
        compiler_params=_params("parallel", "arbitrary"),
        name="sample_a",
    )(page_table, qi_rows, wcol, qda, qdb, *([c_idx] * pp), *([c_ka] * pp), *([c_kb] * pp),
      ki_new, ka_new, kb_new)


def _sample_b_kernel(pt_ref, pa_ref, pb_ref, pan_ref, pbn_ref, *refs, n_pages, nq):
    del pt_ref
    pp = PAGES_PER_STEP
    va_refs, vb_refs = refs[0:pp], refs[pp:2 * pp]
    van_ref, vbn_ref, oa_ref, ob_ref, acca_ref, accb_ref = refs[2 * pp:]
    p = pl.program_id(1)
    rows = N_HEADS * nq

    @pl.when(p == 0)
    def _():
        acca_ref[...] = _dot_nt(van_ref[0], pan_ref[0, 0])
        accb_ref[...] = _dot_nt(vbn_ref[0], pbn_ref[0, 0])

    for v_refs, p_ref, acc_ref in ((va_refs, pa_ref, acca_ref), (vb_refs, pb_ref, accb_ref)):
        v_all = jnp.concatenate([r[0].astype(BF16) for r in v_refs], axis=1)
        p_all = jnp.concatenate([p_ref[0, t] for t in range(pp)], axis=1)
        acc_ref[...] += _dot_nt(v_all, p_all)

    @pl.when(p == n_pages // pp - 1)
    def _():
        same_head = (lax.broadcasted_iota(I32, (WIDTH, rows), 0) // HEAD_DIM
                     == lax.broadcasted_iota(I32, (WIDTH, rows), 1) // nq)
        pick_q = jnp.where(lax.broadcasted_iota(I32, (nq, rows), 1) % nq
                           == lax.broadcasted_iota(I32, (nq, rows), 0), 1.0, 0.0).astype(BF16)
        for acc_ref, o_ref in ((acca_ref, oa_ref), (accb_ref, ob_ref)):
            own = jnp.where(same_head, acc_ref[...], 0.0).astype(o_ref.dtype)
            o_ref[0] = _dot_nt(pick_q, own).astype(o_ref.dtype)


def _sample_b_call(page_table, pa, pb, c_va, c_vb, va_new, vb_new, nq):
    bn, n_pages = page_table.shape
    pp = PAGES_PER_STEP
    rows = N_HEADS * nq
    per_b = lambda shp: pl.BlockSpec((1,) + shp, lambda b, p, pt: (b,) + (0,) * len(shp))
    pcol = pl.BlockSpec((1, pp, rows, LANES), lambda b, p, pt: (b, p, 0, 0))
    pnew = pl.BlockSpec((1, 1, rows, LANES), lambda b, p, pt: (b, n_pages, 0, 0))
    kern = functools.partial(_sample_b_kernel, n_pages=n_pages, nq=nq)
    grid_spec = pltpu.PrefetchScalarGridSpec(
        num_scalar_prefetch=1,
        grid=(bn, n_pages // pp),
        in_specs=[pcol, pcol, pnew, pnew] + _page_specs(WIDTH) + _page_specs(WIDTH)
        + [per_b((WIDTH, LANES)), per_b((WIDTH, LANES))],
        out_specs=[per_b((nq, WIDTH)), per_b((nq, WIDTH))],
        scratch_shapes=[pltpu.VMEM((WIDTH, rows), F32), pltpu.VMEM((WIDTH, rows), F32)],
    )
    return pl.pallas_call(
        kern,
        grid_spec=grid_spec,
        out_shape=[jax.ShapeDtypeStruct((bn, nq, WIDTH), BF16)] * 2,
        compiler_params=_params("parallel", "arbitrary"),
        name="sample_b",
    )(page_table, pa, pb, pa, pb, *([c_va] * pp), *([c_vb] * pp), va_new, vb_new)


def _merge_ffn_kernel(x_ref, ya_ref, yb_ref, ga_ref, gb_ref, wa_ref, wb_ref, wo_ref, gf_ref,
                      wg_ref, wu_ref, wd_ref, gfin_ref, y_ref, *, ff_slab):
    za = _dot(ya_ref[...], wa_ref[...])
    zb = _dot(yb_ref[...], wb_ref[...])
    mix = jax.nn.sigmoid(ga_ref[...]) * za + jax.nn.sigmoid(gb_ref[...]) * zb
    x1 = x_ref[...] + _dot(mix.astype(BF16), wo_ref[...])
    ms = jnp.mean(x1 * x1, axis=-1, keepdims=True)
    xn = (x1 * lax.rsqrt(ms + RMS_EPS) * gf_ref[...]).astype(BF16)
    x2 = x1
    d_ff = wg_ref.shape[1]
    for s0 in range(0, d_ff, ff_slab):
        hg = _dot(xn, wg_ref[:, s0:s0 + ff_slab])
        hu = _dot(xn, wu_ref[:, s0:s0 + ff_slab])
        act = (hg * jax.nn.sigmoid(hg) * hu).astype(BF16)
        x2 = x2 + _dot(act, wd_ref[s0:s0 + ff_slab, :])
    ms2 = jnp.mean(x2 * x2, axis=-1, keepdims=True)
    y_ref[...] = x2 * lax.rsqrt(ms2 + RMS_EPS) * gfin_ref[...]


def _merge_ffn_call(x, ya, yb, ga, gb, wa, wb, wo, gf, wg, wu, wd, gfin, tm):
    m, d = x.shape
    d_ff = wg.shape[1]
    ff_slab = d_ff // 2 if (d_ff // 2) % LANES == 0 else d_ff
    row = lambda w: pl.BlockSpec((tm, w), lambda i: (i, 0))
    kern = functools.partial(_merge_ffn_kernel, ff_slab=ff_slab)
    return pl.pallas_call(
        kern,
        grid=(m // tm,),
        in_specs=[row(d), row(WIDTH), row(WIDTH), row(d), row(d),
                  _const_spec(wa.shape), _const_spec(wb.shape), _const_spec(wo.shape), _const_spec((1, d)),
                  _const_spec(wg.shape), _const_spec(wu.shape), _const_spec(wd.shape), _const_spec((1, d))],
        out_specs=row(d),
        out_shape=jax.ShapeDtypeStruct((m, d), F32),
        compiler_params=_params("parallel"),
        name="merge_ffn",
    )(x, ya, yb, ga, gb, wa, wb, wo, gf.reshape(1, d), wg, wu, wd, gfin.reshape(1, d))


def _layer(x_p, x_s, cache_ka, cache_va, cache_idx_k, cache_kb, cache_vb, page_table,
           g_attn, w_in, g_idx_k, w_proj_a, w_proj_b, w_out, g_ffn, w_gate, w_up, w_down, g_final):
    bn, s, d = x_p.shape
    db, nq, _ = x_s.shape
    n_pages = page_table.shape[1]
    n_pool = cache_ka.shape[0]
    past = n_pages * PAGE_SIZE
    assert s % PROJ_ROWS == 0 and Q_TILE == MOBA_BLOCK and PROJ_ROWS % MOBA_BLOCK == 0

    w_arr = _arrange_w_in(w_in)
    gk = jnp.pad(g_idx_k, (0, LANES - IDX_DIM)).reshape(1, LANES)
    wa, wb, wo = w_proj_a.astype(BF16), w_proj_b.astype(BF16), w_out.astype(BF16)
    wg, wu, wd = w_gate.astype(BF16), w_up.astype(BF16), w_down.astype(BF16)

    cs_p, sn_p = _rope_tables(jnp.arange(s))
    (qat, qit, qbt, kab, kbb, vat4, vbt4, kat, vat, kbt, vbt, ga, gb, kwb, kwt, kmean) = _proj_prompt_call(
        x_p.reshape(bn * s, d), g_attn, w_arr, cs_p, sn_p, gk, bn, s)
    ya = _dsa_prompt_call(qat, qit, kwt, kwb, kab, vat4, bn, s)
    nb = s // MOBA_BLOCK
    kmean_b = jnp.pad(kmean.reshape(bn, nb, WIDTH), ((0, 0), (0, -nb % SUBLANES), (0, 0)))
    yb = _moba_prompt_call(qbt, kmean_b, kbb, vbt4, bn, s)
    y_p = _merge_ffn_call(x_p.reshape(bn * s, d), ya, yb, ga, gb, wa, wb, wo, g_ffn, wg, wu, wd, g_final,
                          FFN_ROWS)
    heads_out = lambda t: t.reshape(bn, N_HEADS, HEAD_DIM, s).transpose(0, 3, 1, 2)
    new_p = (heads_out(kat), heads_out(vat), kwt[:, :IDX_DIM, :].transpose(0, 2, 1), heads_out(kbt), heads_out(vbt))

    ms_ = db * nq
    cs_s, sn_s = _rope_tables(past + jnp.arange(nq))
    cs_s, sn_s = jnp.tile(cs_s, (db, 1)), jnp.tile(sn_s, (db, 1))
    (qa_s, qi_s, qb_s, kaf_s, vaf_s, kbf_s, vbf_s, kab_s, vab_s, kbb_s, vbb_s, ga_s, gb_s, kwf_s, kwb_s) = \
        _proj_sample_call(x_s.reshape(ms_, d), g_attn, w_arr, cs_s, sn_s, gk)
    heads = lambda t: t.reshape(db, nq, N_HEADS, HEAD_DIM).transpose(0, 2, 1, 3)
    eye = jnp.eye(N_HEADS, dtype=BF16)
    blockdiag = lambda t: jnp.einsum("bhqd,hg->bhqgd", heads(t), eye).reshape(db, N_HEADS * nq, WIDTH)
    qi_rows = heads(qi_s).reshape(db, N_HEADS * nq, IDX_DIM)
    wi_s = kwf_s[:, IDX_DIM:IDX_DIM + N_IDX_HEADS].reshape(db, nq, N_IDX_HEADS).transpose(0, 2, 1)
    wcol = jnp.broadcast_to(wi_s.reshape(db, N_IDX_HEADS * nq, 1), (db, N_IDX_HEADS * nq, LANES))
    new_t = lambda t: jnp.pad(t.reshape(db, nq, t.shape[-1]).transpose(0, 2, 1), ((0, 0), (0, 0), (0, LANES - nq)))
    page_t = lambda c: c.transpose(0, 2, 3, 1).reshape(n_pool, WIDTH, PAGE_SIZE)
    pa, pb = _sample_a_call(
        page_table, qi_rows, wcol, blockdiag(qa_s), blockdiag(qb_s),
        cache_idx_k.transpose(0, 2, 1), page_t(cache_ka), page_t(cache_kb),
        new_t(kwb_s[:, :IDX_DIM]), new_t(kab_s), new_t(kbb_s), nq)
    ya_s, yb_s = _sample_b_call(page_table, pa, pb, page_t(cache_va), page_t(cache_vb),
                                new_t(vab_s), new_t(vbb_s), nq)
    y_s = _merge_ffn_call(x_s.reshape(ms_, d), ya_s.reshape(ms_, WIDTH), yb_s.reshape(ms_, WIDTH), ga_s, gb_s,
                          wa, wb, wo, g_ffn, wg, wu, wd, g_final, ms_)
    new_s = (kaf_s.reshape(db, nq, N_HEADS, HEAD_DIM), vaf_s.reshape(db, nq, N_HEADS, HEAD_DIM),
             kwf_s[:, :IDX_DIM].reshape(db, nq, IDX_DIM),
             kbf_s.reshape(db, nq, N_HEADS, HEAD_DIM), vbf_s.reshape(db, nq, N_HEADS, HEAD_DIM))
    return y_p.reshape(bn, s, d), y_s.reshape(db, nq, d), new_p, new_s


def kernel(x_prompt, x_sample, cache_ka, cache_va, cache_idx_k, cache_kb, cache_vb, page_table,
           g_attn, w_in, g_idx_k, w_proj_a, w_proj_b, w_out, g_ffn, w_gate, w_up, w_down, g_final):
    assert w_in.shape[0] == 1, "the fused final norm assumes a single layer"
    y_p, y_s, new_p, new_s = _layer(
        x_prompt, x_sample, cache_ka[0], cache_va[0], cache_idx_k[0], cache_kb[0], cache_vb[0], page_table,
        g_attn[0], w_in[0], g_idx_k[0], w_proj_a[0], w_proj_b[0], w_out[0], g_ffn[0],
        w_gate[0], w_up[0], w_down[0], g_final)
    return (y_p, y_s) + tuple(t[None] for t in new_p) + tuple(t[None] for t in new_s)
```

```python
import functools

import jax
import jax.numpy as jnp
from jax import lax
from jax.experimental import pallas as pl
from jax.experimental.pallas import tpu as pltpu

F32 = jnp.float32
BF16 = jnp.bfloat16
I32 = jnp.int32

HEAD_DIM = 64
N_HEADS = 8
WIDTH = N_HEADS * HEAD_DIM
IDX_DIM = 64
N_IDX_HEADS = 8
TOPK_KEYS = 256
MOBA_BLOCK = 256
MOBA_TOPK = 3
PAGE_SIZE = 128
ROPE_THETA = 10000.0
RMS_EPS = 1e-6
NEG = -1e30
LOG2_E = 1.4426950408889634
LANES = 128
SUBLANES = 8
INT_MIN = -(2 ** 31)
VMEM_LIMIT = 56 * 1024 * 1024
Q_TILE = 256
DSA_CHUNK = 256
PROJ_ROWS = 512
FFN_ROWS = 512
PAGES_PER_STEP = 16

_NT = (((1,), (1,)), ((), ()))


def _dot_nt(a, b):
    return lax.dot_general(a, b, _NT, preferred_element_type=F32)


def _dot(a, b):
    return jnp.dot(a, b, preferred_element_type=F32)


def _head(h):
    return slice(h * HEAD_DIM, (h + 1) * HEAD_DIM)


def _params(*sem):
    return pltpu.CompilerParams(dimension_semantics=sem, vmem_limit_bytes=VMEM_LIMIT)


def _const_spec(shape):
    nd = len(shape)
    return pl.BlockSpec(shape, lambda *_: (0,) * nd, pipeline_mode=pl.Buffered(1))


def _rope_tables(pos):
    inv_freq = ROPE_THETA ** (-jnp.arange(0, HEAD_DIM, 2, dtype=F32) / HEAD_DIM)
    ang = pos.astype(F32)[:, None] * inv_freq[None, :]
    cos, sin = jnp.cos(ang), jnp.sin(ang)
    return (jnp.concatenate([cos, cos, cos, cos], axis=-1),
            jnp.concatenate([-sin, sin, -sin, sin], axis=-1))


def _proj_kernel(x_ref, g_ref, w_ref, cs_ref, sn_ref, gk_ref, *outs, transposed, va_tile, vb_tile):
    tm = x_ref.shape[0]
    x = x_ref[...]
    ms = jnp.mean(x * x, axis=-1, keepdims=True)
    xn = (x * lax.rsqrt(ms + RMS_EPS) * g_ref[...]).astype(BF16)
    cs = cs_ref[...]
    sn = sn_ref[...]
    lane = lax.broadcasted_iota(I32, (tm, LANES), 1)
    first_half = (lane & (HEAD_DIM // 2)) == 0

    def rope(p):
        swapped = jnp.where(first_half, pltpu.roll(p, LANES - HEAD_DIM // 2, 1),
                            pltpu.roll(p, HEAD_DIM // 2, 1))
        return p * cs + swapped * sn

    def slab(j, width=WIDTH):
        return _dot(xn, w_ref[:, j * WIDTH:j * WIDTH + width])

    def rope_slab(h):
        return jnp.concatenate(
            [rope(h[:, k * LANES:(k + 1) * LANES]) for k in range(WIDTH // LANES)], axis=1)

    qa, qi, qb = rope_slab(slab(0)), rope_slab(slab(1)), rope_slab(slab(2))
    ka, va, kb, vb = rope_slab(slab(3)), slab(4), rope_slab(slab(5)), slab(6)
    t = slab(11, LANES)
    is_key = lane < IDX_DIM
    kms = jnp.sum(jnp.where(is_key, t * t, 0.0), axis=-1, keepdims=True) * (1.0 / IDX_DIM)
    kn = rope(t * lax.rsqrt(kms + RMS_EPS) * gk_ref[...])
    kw = jnp.where(is_key, kn, jnp.where(lane < IDX_DIM + N_IDX_HEADS, t * (N_IDX_HEADS ** -0.5), 0.0))

    if transposed:
        (qat_ref, qit_ref, qbt_ref, kab_ref, kbb_ref, vat4_ref, vbt4_ref,
         kat_ref, vat_ref, kbt_ref, vbt_ref, ga_ref, gb_ref, kwb_ref, kwt_ref, kmean_ref) = outs
        qat_ref[0] = qa.T.astype(BF16)
        qit_ref[0] = qi.T.astype(BF16)
        qbt_ref[0] = qb.T.astype(BF16)
        kab_ref[...] = ka.astype(BF16)
        kbb_ref[...] = kb.astype(BF16)
        kat_ref[0] = ka.T
        kbt_ref[0] = kb.T
        va_t, vb_t = va.T, vb.T
        vat_ref[0] = va_t
        vbt_ref[0] = vb_t
        for r in range(tm // va_tile):
            vat4_ref[0, r] = va_t[:, r * va_tile:(r + 1) * va_tile].astype(BF16)
        for r in range(tm // vb_tile):
            vbt4_ref[0, r] = vb_t[:, r * vb_tile:(r + 1) * vb_tile].astype(BF16)
        kwb_ref[...] = kw.astype(BF16)
        kwt_ref[0] = kw.T
        for r in range(tm // MOBA_BLOCK):
            kmean_ref[0, r:r + 1, :] = jnp.mean(kb[r * MOBA_BLOCK:(r + 1) * MOBA_BLOCK], axis=0, keepdims=True)
    else:
        (qa_ref, qi_ref, qb_ref, kaf_ref, vaf_ref, kbf_ref, vbf_ref,
         kab_ref, vab_ref, kbb_ref, vbb_ref, ga_ref, gb_ref, kwf_ref, kwb_ref) = outs
        qa_ref[...] = qa.astype(BF16)
        qi_ref[...] = qi.astype(BF16)
        qb_ref[...] = qb.astype(BF16)
        for f_ref, b_ref, val in ((kaf_ref, kab_ref, ka), (vaf_ref, vab_ref, va),
                                  (kbf_ref, kbb_ref, kb), (vbf_ref, vbb_ref, vb)):
            f_ref[...] = val
            b_ref[...] = val.astype(BF16)
        kwf_ref[...] = kw
        kwb_ref[...] = kw.astype(BF16)
    ga_ref[:, :WIDTH] = slab(7)
    ga_ref[:, WIDTH:] = slab(8)
    gb_ref[:, :WIDTH] = slab(9)
    gb_ref[:, WIDTH:] = slab(10)


def _arrange_w_in(w_in):
    d = w_in.shape[0]
    o = [0]
    for s in (WIDTH, WIDTH, WIDTH, N_IDX_HEADS * IDX_DIM, IDX_DIM, N_IDX_HEADS, WIDTH, WIDTH, WIDTH, d, d):
        o.append(o[-1] + s)
    qa, ka, va, qi, ki, wi, qb, kb, vb, ga, gb = [w_in[:, o[i]:o[i + 1]] for i in range(11)]
    qscale = HEAD_DIM ** -0.5 * LOG2_E
    iscale = IDX_DIM ** -0.5
    pad = jnp.zeros((d, LANES - IDX_DIM - N_IDX_HEADS), w_in.dtype)
    return jnp.concatenate([qa * qscale, qi * iscale, qb * qscale, ka, va, kb, vb, ga, gb, ki, wi, pad],
                           axis=1).astype(BF16)


def _proj_prompt_call(x, g_attn, w_arr, cs, sn, gk, bn, s):
    m, d = x.shape
    tm = PROJ_ROWS
    spt = s // tm
    row = lambda w: pl.BlockSpec((tm, w), lambda i: (i, 0))
    tab = pl.BlockSpec((tm, LANES), lambda i: (i % spt, 0))
    col = lambda r: pl.BlockSpec((1, r, tm), lambda i: (i // spt, 0, i % spt))
    tile4 = lambda w: pl.BlockSpec((1, tm // w, WIDTH, w), lambda i: (i // spt, i % spt, 0, 0))
    t_shape = lambda r, dt: jax.ShapeDtypeStruct((bn, r, s), dt)
    out_shape = (
        [t_shape(WIDTH, BF16)] * 3
        + [jax.ShapeDtypeStruct((m, WIDTH), BF16)] * 2
        + [jax.ShapeDtypeStruct((bn, s // DSA_CHUNK, WIDTH, DSA_CHUNK), BF16),
           jax.ShapeDtypeStruct((bn, s // MOBA_BLOCK, WIDTH, MOBA_BLOCK), BF16)]
        + [t_shape(WIDTH, F32)] * 4
        + [jax.ShapeDtypeStruct((m, d), F32)] * 2
        + [jax.ShapeDtypeStruct((m, LANES), BF16), t_shape(LANES, F32),
           jax.ShapeDtypeStruct((m // tm, tm // MOBA_BLOCK, WIDTH), F32)])
    out_specs = ([col(WIDTH)] * 3 + [row(WIDTH)] * 2 + [tile4(DSA_CHUNK), tile4(MOBA_BLOCK)]
                 + [col(WIDTH)] * 4 + [row(d)] * 2 + [row(LANES), col(LANES),
                 pl.BlockSpec((1, tm // MOBA_BLOCK, WIDTH), lambda i: (i, 0, 0))])
    kern = functools.partial(_proj_kernel, transposed=True, va_tile=DSA_CHUNK, vb_tile=MOBA_BLOCK)
    return pl.pallas_call(
        kern,
        grid=(m // tm,),
        in_specs=[row(d), _const_spec((1, d)), _const_spec(w_arr.shape), tab, tab, _const_spec((1, LANES))],
        out_specs=out_specs,
        out_shape=out_shape,
        compiler_params=_params("parallel"),
        name="proj",
    )(x, g_attn.reshape(1, d), w_arr, cs, sn, gk)


def _proj_sample_call(x, g_attn, w_arr, cs, sn, gk):
    m, d = x.shape
    row = lambda w: pl.BlockSpec((m, w), lambda i: (0, 0))
    out_shape = ([jax.ShapeDtypeStruct((m, WIDTH), BF16)] * 3 + [jax.ShapeDtypeStruct((m, WIDTH), F32)] * 4
                 + [jax.ShapeDtypeStruct((m, WIDTH), BF16)] * 4 + [jax.ShapeDtypeStruct((m, d), F32)] * 2
                 + [jax.ShapeDtypeStruct((m, LANES), F32), jax.ShapeDtypeStruct((m, LANES), BF16)])
    kern = functools.partial(_proj_kernel, transposed=False, va_tile=0, vb_tile=0)
    return pl.pallas_call(
        kern,
        grid=(1,),
        in_specs=[row(d), _const_spec((1, d)), _const_spec(w_arr.shape), row(LANES), row(LANES),
                  _const_spec((1, LANES))],
        out_specs=[row(WIDTH)] * 11 + [row(d)] * 2 + [row(LANES)] * 2,
        out_shape=out_shape,
        compiler_params=_params("arbitrary"),
        name="proj_sample",
    )(x, g_attn.reshape(1, d), w_arr, cs, sn, gk)


def _key_to_float(key):
    return lax.bitcast_convert_type(key ^ ((key >> 31) & jnp.int32(0x7FFFFFFF)), F32)


def _select_threshold(count, state_shape, bc, k_sel, idx_bits, y_ref, unroll=False):
    kf = float(k_sel)
    rep = lambda v: jnp.broadcast_to(v, state_shape)

    def bis(t, key):
        cand = key + lax.shift_left(jnp.int32(1), 31 - t)
        cf = bc(_key_to_float(cand))
        return jnp.where(rep(count(lambda x, _: x >= cf)) >= kf, cand, key)

    thr = _key_to_float(lax.fori_loop(0, 32, bis, jnp.full(state_shape, INT_MIN, I32), unroll=unroll))
    tb = bc(thr)
    c_ge = count(lambda x, _: x >= tb)
    y_ref[...] = jnp.full(state_shape, 2 ** 30, I32)

    @pl.when(jnp.max(c_ge) > kf)
    def _():
        need = rep(kf - count(lambda x, _: x > tb))

        def ybis(t, y):
            cand = y + lax.shift_left(jnp.int32(1), idx_bits - 1 - t)
            cb = bc(cand)
            f = count(lambda x, idx: (x == tb) & (idx < cb))
            return jnp.where(rep(f) < need, cand, y)

        y_ref[...] = lax.fori_loop(0, idx_bits, ybis, jnp.zeros(state_shape, I32))

    return thr


def _top_blocks(gate, valid, n_top, axis):
    n = gate.shape[axis]
    ids = lax.broadcasted_iota(I32, gate.shape, axis).astype(F32)
    g = jnp.where(valid, gate, NEG)
    marks = jnp.zeros(gate.shape, F32)
    for _ in range(n_top):
        mx = jnp.max(g, axis=axis, keepdims=True)
        first = jnp.min(jnp.where(g == mx, ids, float(n)), axis=axis, keepdims=True)
        hit = ids == first
        marks = jnp.where(hit, 1.0, marks)
        g = jnp.where(hit, -jnp.inf, g)
    return jnp.where(valid, marks, 0.0)


def _attend_chunk_t(kc, qt_ref, vt, bias, pick, lg_ref, p_ref, m_ref, l_ref, acc_ref):
    for h in range(N_HEADS):
        lg = _dot(kc[:, _head(h)], qt_ref[0, _head(h), :])
        lg_ref[h] = lg if bias is None else lg + bias(h)
    for h in range(N_HEADS):
        lg = lg_ref[h]
        m_old = m_ref[h]
        mx = jnp.max(lg, axis=0, keepdims=True)
        if pick is None:
            m_new = jnp.maximum(m_old, mx)
            shift = m_new[0:1]
        else:
            m_new = jnp.maximum(m_old, jnp.where(pick(h), mx, NEG))
            shift = jnp.where(pick(h), m_new[0:1], -NEG)
        alpha = jnp.exp2(m_old - m_new)
        p = jnp.exp2(lg - shift)
        l_ref[h] = alpha * l_ref[h] + jnp.sum(p, axis=0, keepdims=True)
        m_ref[h] = m_new
        p_ref[h] = p.astype(BF16)
        acc_ref[_head(h), :] = alpha[0:1] * acc_ref[_head(h), :]
    for h in range(N_HEADS):
        acc_ref[_head(h), :] += _dot(vt(h), p_ref[h])


def _init_softmax_state(m_ref, l_ref, acc_ref):
    m_ref[...] = jnp.full(m_ref.shape, NEG, F32)
    l_ref[...] = jnp.zeros(l_ref.shape, F32)
    acc_ref[...] = jnp.zeros(acc_ref.shape, F32)


def _finish_heads_t(o_ref, l_ref, acc_ref):
    for h in range(N_HEADS):
        acc_ref[_head(h), :] = acc_ref[_head(h), :] / l_ref[h][0:1]
    o_ref[...] = acc_ref[...].T.astype(o_ref.dtype)


def _attn_scratch(c_w, qt):
    return [pltpu.VMEM((N_HEADS, c_w, qt), F32), pltpu.VMEM((N_HEADS, c_w, qt), BF16),
            pltpu.VMEM((N_HEADS, SUBLANES, qt), F32), pltpu.VMEM((N_HEADS, SUBLANES, qt), F32),
            pltpu.VMEM((WIDTH, qt), F32)]


def _dsa_prompt_kernel(qat_ref, qit_ref, kwt_ref, kw_ref, ka_ref, vt_ref, o_ref,
                       sc_ref, y_ref, lg_ref, p_ref, m_ref, l_ref, acc_ref, *, k_sel, idx_bits):
    i = pl.program_id(1)
    _, c_w, qt = sc_ref.shape
    nkc = ((i + 1) * qt + c_w - 1) // c_w
    row = lax.broadcasted_iota(I32, (c_w, qt), 0)
    qpos = i * qt + lax.broadcasted_iota(I32, (c_w, qt), 1)

    def score_body(c, carry):
        k0 = pl.multiple_of(c * c_w, c_w)
        ki = kw_ref[pl.ds(k0, c_w), :][:, :IDX_DIM]
        acc = jnp.zeros((c_w, qt), F32)
        for h in range(N_IDX_HEADS):
            s = _dot(ki, qit_ref[0, h * IDX_DIM:(h + 1) * IDX_DIM, :])
            acc = acc + jnp.maximum(s, 0.0) * kwt_ref[0, IDX_DIM + h:IDX_DIM + h + 1, :]
        sc_ref[c] = jnp.where((k0 + row) <= qpos, acc, NEG)
        return carry

    lax.fori_loop(0, nkc, score_body, 0)

    @pl.when(nkc % 2 == 1)
    def _():
        sc_ref[nkc] = jnp.full((c_w, qt), NEG, F32)

    def count(pred):
        def body(c2, cnt):
            for c in (2 * c2, 2 * c2 + 1):
                hit = jnp.where(pred(sc_ref[c], c * c_w + row), 1.0, 0.0)
                cnt = cnt + jnp.sum(hit.reshape(c_w // SUBLANES, SUBLANES, qt), axis=0)
            return cnt

        cnt = lax.fori_loop(0, (nkc + 1) // 2, body, jnp.zeros((SUBLANES, qt), F32))
        return jnp.sum(cnt, axis=0, keepdims=True)

    thr = _select_threshold(count, (SUBLANES, qt), lambda v: v[0:1], k_sel, idx_bits, y_ref)
    tb = thr[0:1]
    yb = y_ref[0:1, :]
    _init_softmax_state(m_ref, l_ref, acc_ref)

    def att_body(c, carry):
        k0 = pl.multiple_of(c * c_w, c_w)
        x = sc_ref[c]
        kidx = k0 + row
        sel = ((x > tb) | ((x == tb) & (kidx <= yb))) & (kidx <= qpos)
        bias = jnp.where(sel, 0.0, NEG)
        _attend_chunk_t(ka_ref[pl.ds(k0, c_w), :], qat_ref, lambda h: vt_ref[0, c, _head(h), :],
                        lambda h: bias, None, lg_ref, p_ref, m_ref, l_ref, acc_ref)
        return carry

    lax.fori_loop(0, nkc, att_body, 0)
    _finish_heads_t(o_ref, l_ref, acc_ref)


def _dsa_prompt_call(qat, qit, kwt, kwb, kab, vat4, bn, s):
    qt, c_w = Q_TILE, DSA_CHUNK
    assert (s // c_w) % 2 == 0
    nq = s // qt
    k_sel = min(TOPK_KEYS, s // 4)
    idx_bits = int(s).bit_length()
    qcol = lambda r: pl.BlockSpec((1, r, qt), lambda b, i: (b, 0, i))
    seq = lambda w: pl.BlockSpec((s, w), lambda b, i: (b, 0))
    kern = functools.partial(_dsa_prompt_kernel, k_sel=k_sel, idx_bits=idx_bits)
    return pl.pallas_call(
        kern,
        grid=(bn, nq),
        in_specs=[qcol(WIDTH), qcol(WIDTH), qcol(LANES), seq(LANES), seq(WIDTH),
                  pl.BlockSpec((1, s // c_w, WIDTH, c_w), lambda b, i: (b, 0, 0, 0))],
        out_specs=pl.BlockSpec((qt, WIDTH), lambda b, i: (b * nq + i, 0)),
        out_shape=jax.ShapeDtypeStruct((bn * s, WIDTH), BF16),
        scratch_shapes=[pltpu.VMEM((s // c_w, c_w, qt), F32), pltpu.VMEM((SUBLANES, qt), I32)]
        + _attn_scratch(c_w, qt),
        compiler_params=_params("parallel", "arbitrary"),
        name="dsa_prompt",
    )(qat, qit, kwt, kwb, kab, vat4)


def _moba_prompt_kernel(qbt_ref, km_ref, kb_ref, vt_ref, o_ref, marks_ref, lg_ref, p_ref, m_ref, l_ref, acc_ref):
    cur = pl.program_id(1)
    nb, qt = marks_ref.shape[1:]
    km = km_ref[0].astype(BF16)
    past = lax.broadcasted_iota(I32, (nb, qt), 0) < cur
    for h in range(N_HEADS):
        marks_ref[h] = _top_blocks(_dot(km[:, _head(h)], qbt_ref[0, _head(h), :]), past, MOBA_TOPK, 0)
    _init_softmax_state(m_ref, l_ref, acc_ref)

    def block(j, bias, pick):
        kc = kb_ref[pl.ds(pl.multiple_of(j * MOBA_BLOCK, MOBA_BLOCK), MOBA_BLOCK), :]
        _attend_chunk_t(kc, qbt_ref, lambda h: vt_ref[0, j, _head(h), :], bias, pick,
                        lg_ref, p_ref, m_ref, l_ref, acc_ref)

    def blk_body(j, carry):
        block(j, None, lambda h: marks_ref[h, pl.ds(j, 1), :] > 0.5)
        return carry

    lax.fori_loop(0, cur, blk_body, 0)
    causal = (lax.broadcasted_iota(I32, (MOBA_BLOCK, qt), 0) <= lax.broadcasted_iota(I32, (MOBA_BLOCK, qt), 1))
    own_bias = jnp.where(causal, 0.0, NEG)
    block(cur, lambda h: own_bias, None)
    _finish_heads_t(o_ref, l_ref, acc_ref)


def _moba_prompt_call(qbt, kmean, kbb, vbt4, bn, s):
    qt = MOBA_BLOCK
    nq = s // qt
    nb = kmean.shape[1]
    return pl.pallas_call(
        _moba_prompt_kernel,
        grid=(bn, nq),
        in_specs=[pl.BlockSpec((1, WIDTH, qt), lambda b, i: (b, 0, i)),
                  pl.BlockSpec((1, nb, WIDTH), lambda b, i: (b, 0, 0)),
                  pl.BlockSpec((s, WIDTH), lambda b, i: (b, 0)),
                  pl.BlockSpec((1, nq, WIDTH, qt), lambda b, i: (b, 0, 0, 0))],
        out_specs=pl.BlockSpec((qt, WIDTH), lambda b, i: (b * nq + i, 0)),
        out_shape=jax.ShapeDtypeStruct((bn * s, WIDTH), BF16),
        scratch_shapes=[pltpu.VMEM((N_HEADS, nb, qt), F32)] + _attn_scratch(MOBA_BLOCK, qt),
        compiler_params=_params("parallel", "arbitrary"),
        name="moba_prompt",
    )(qbt, kmean, kbb, vbt4)


def _sample_a_kernel(pt_ref, qi_ref, wcol_ref, qda_ref, qdb_ref, *refs, n_pages, nq, k_sel, idx_bits):
    del pt_ref
    pp = PAGES_PER_STEP
    idx_refs, ka_refs, kb_refs = refs[0:pp], refs[pp:2 * pp], refs[2 * pp:3 * pp]
    kin_ref, kan_ref, kbn_ref, pa_ref, pb_ref, sc_ref, y_ref, la_ref, lb_ref = refs[3 * pp:]
    p = pl.program_id(1)
    nch = n_pages + 1
    rows = N_HEADS * nq
    lane = lax.broadcasted_iota(I32, (rows, LANES), 1)
    qrow = lax.broadcasted_iota(I32, (rows, LANES), 0) % nq

    def idx_scores(k_t):
        s = jnp.maximum(_dot(qi_ref[0], k_t), 0.0) * wcol_ref[0]
        sc = s[0:nq]
        for h in range(1, N_IDX_HEADS):
            sc = sc + s[h * nq:(h + 1) * nq]
        return sc

    for t in range(pp):
        g = p * pp + t
        sc_ref[g] = idx_scores(idx_refs[t][0].astype(BF16))
        la_ref[g] = _dot(qda_ref[0], ka_refs[t][0].astype(BF16))
        lb_ref[g] = _dot(qdb_ref[0], kb_refs[t][0].astype(BF16))

    def softmax_out(l_ref, p_ref, keep):
        lg = jnp.where(keep, l_ref[...].reshape(nch, N_HEADS, nq, LANES), NEG)
        m = jnp.max(jnp.max(lg, axis=0), axis=-1, keepdims=True)
        e = jnp.exp2(lg - m)
        inv = 1.0 / jnp.sum(jnp.sum(e, axis=0), axis=-1, keepdims=True)
        p_ref[0] = (e * inv).reshape(nch, rows, LANES).astype(p_ref.dtype)

    @pl.when(p == n_pages // pp - 1)
    def _():
        new_ok = (lane <= qrow) & (lane < nq)
        kidx = (lax.broadcasted_iota(I32, (nch, nq, LANES), 0) * LANES
                + lax.broadcasted_iota(I32, (nch, nq, LANES), 2))
        cpb = MOBA_BLOCK // PAGE_SIZE
        n_blk = n_pages // cpb
        lb_ref[n_pages] = jnp.where(new_ok, _dot(qdb_ref[0], kbn_ref[0]), NEG)
        gate = jnp.zeros((rows, LANES), F32)
        for j in range(n_blk):
            tot = lb_ref[j * cpb]
            for c in range(1, cpb):
                tot = tot + lb_ref[j * cpb + c]
            gate = jnp.where(lane == j, jnp.sum(tot, axis=-1, keepdims=True) * (1.0 / MOBA_BLOCK), gate)
        marks = _top_blocks(gate, lane < n_blk, MOBA_TOPK, 1)
        cols = [jnp.sum(jnp.where(lane == j, marks, 0.0), axis=-1, keepdims=True) for j in range(n_blk)]
        keep = jnp.stack([jnp.broadcast_to(cols[c // cpb], (rows, LANES)) for c in range(n_pages)]
                         + [jnp.ones((rows, LANES), F32)], axis=0)
        softmax_out(lb_ref, pb_ref, keep.reshape(nch, N_HEADS, nq, LANES) > 0.5)
        sc_ref[n_pages] = jnp.where(new_ok[:nq], idx_scores(kin_ref[0]), NEG)
        sc = sc_ref[...]

        def count(pred):
            return jnp.sum(jnp.sum(jnp.where(pred(sc, kidx), 1.0, 0.0), axis=0), axis=-1, keepdims=True)

        thr = _select_threshold(count, (nq, LANES), lambda v: v, k_sel, idx_bits, y_ref, unroll=True)
        sel = (sc > thr) | ((sc == thr) & (kidx <= y_ref[...]))
        la_ref[n_pages] = jnp.where(new_ok, _dot(qda_ref[0], kan_ref[0]), NEG)
        softmax_out(la_ref, pa_ref, sel[:, None])


def _page_specs(rows):
    pp = PAGES_PER_STEP
    return [pl.BlockSpec((1, rows, PAGE_SIZE), functools.partial(
        lambda b, p, pt, t: (pt[b, p * pp + t], 0, 0), t=t)) for t in range(pp)]


def _sample_a_call(page_table, qi_rows, wcol, qda, qdb, c_idx, c_ka, c_kb, ki_new, ka_new, kb_new, nq):
    bn, n_pages = page_table.shape
    pp = PAGES_PER_STEP
    assert n_pages % pp == 0 and MOBA_BLOCK % PAGE_SIZE == 0
    rows = N_HEADS * nq
    nch = n_pages + 1
    k_sel = min(TOPK_KEYS, (n_pages * PAGE_SIZE + nq) // 4)
    idx_bits = int(nch * LANES).bit_length()
    per_b = lambda shp: pl.BlockSpec((1,) + shp, lambda b, p, pt: (b,) + (0,) * len(shp))
    kern = functools.partial(_sample_a_kernel, n_pages=n_pages, nq=nq, k_sel=k_sel, idx_bits=idx_bits)
    grid_spec = pltpu.PrefetchScalarGridSpec(
        num_scalar_prefetch=1,
        grid=(bn, n_pages // pp),
        in_specs=[per_b((rows, IDX_DIM)), per_b((rows, LANES)), per_b((rows, WIDTH)), per_b((rows, WIDTH))]
        + _page_specs(IDX_DIM) + _page_specs(WIDTH) + _page_specs(WIDTH)
        + [per_b((IDX_DIM, LANES)), per_b((WIDTH, LANES)), per_b((WIDTH, LANES))],
        out_specs=[per_b((nch, rows, LANES)), per_b((nch, rows, LANES))],
        scratch_shapes=[
            pltpu.VMEM((nch, nq, LANES), F32),
            pltpu.VMEM((nq, LANES), I32),
            pltpu.VMEM((nch, rows, LANES), F32),
            pltpu.VMEM((nch, rows, LANES), F32),
        ],
    )
    return pl.pallas_call(
        kern,
        grid_spec=grid_spec,
        out_shape=[jax.ShapeDtypeStruct((bn, nch, rows, LANES), BF16)] * 2,
        compiler_params=_params("parallel", "arbitrary"),
        name="sample_a",
    )(page_table, qi_rows, wcol, qda, qdb, *([c_idx] * pp), *([c_ka] * pp), *([c_kb] * pp),
      ki_new, ka_new, kb_new)


def _sample_b_kernel(pt_ref, pa_ref, pb_ref, pan_ref, pbn_ref, *refs, n_pages, nq):
    del pt_ref
    pp = PAGES_PER_STEP
    va_refs, vb_refs = refs[0:pp], refs[pp:2 * pp]
    van_ref, vbn_ref, oa_ref, ob_ref, acca_ref, accb_ref = refs[2 * pp:]
    p = pl.program_id(1)
    rows = N_HEADS * nq

    @pl.when(p == 0)
    def _():
        acca_ref[...] = _dot_nt(van_ref[0], pan_ref[0, 0])
        accb_ref[...] = _dot_nt(vbn_ref[0], pbn_ref[0, 0])

    for v_refs, p_ref, acc_ref in ((va_refs, pa_ref, acca_ref), (vb_refs, pb_ref, accb_ref)):
        v_all = jnp.concatenate([r[0].astype(BF16) for r in v_refs], axis=1)
        p_all = jnp.concatenate([p_ref[0, t] for t in range(pp)], axis=1)
        acc_ref[...] += _dot_nt(v_all, p_all)

    @pl.when(p == n_pages // pp - 1)
    def _():
        same_head = (lax.broadcasted_iota(I32, (WIDTH, rows), 0) // HEAD_DIM
                     == lax.broadcasted_iota(I32, (WIDTH, rows), 1) // nq)
        pick_q = jnp.where(lax.broadcasted_iota(I32, (nq, rows), 1) % nq
                           == lax.broadcasted_iota(I32, (nq, rows), 0), 1.0, 0.0).astype(BF16)
        for acc_ref, o_ref in ((acca_ref, oa_ref), (accb_ref, ob_ref)):
            own = jnp.where(same_head, acc_ref[...], 0.0).astype(o_ref.dtype)
            o_ref[0] = _dot_nt(pick_q, own).astype(o_ref.dtype)


def _sample_b_call(page_table, pa, pb, c_va, c_vb, va_new, vb_new, nq):
    bn, n_pages = page_table.shape
    pp = PAGES_PER_STEP
    rows = N_HEADS * nq
    per_b = lambda shp: pl.BlockSpec((1,) + shp, lambda b, p, pt: (b,) + (0,) * len(shp))
    pcol = pl.BlockSpec((1, pp, rows, LANES), lambda b, p, pt: (b, p, 0, 0))
    pnew = pl.BlockSpec((1, 1, rows, LANES), lambda b, p, pt: (b, n_pages, 0, 0))
    kern = functools.partial(_sample_b_kernel, n_pages=n_pages, nq=nq)
    grid_spec = pltpu.PrefetchScalarGridSpec(
        num_scalar_prefetch=1,
        grid=(bn, n_pages // pp),
        in_specs=[pcol, pcol, pnew, pnew] + _page_specs(WIDTH) + _page_specs(WIDTH)
        + [per_b((WIDTH, LANES)), per_b((WIDTH, LANES))],
        out_specs=[per_b((nq, WIDTH)), per_b((nq, WIDTH))],
        scratch_shapes=[pltpu.VMEM((WIDTH, rows), F32), pltpu.VMEM((WIDTH, rows), F32)],
    )
    return pl.pallas_call(
        kern,
        grid_spec=grid_spec,
        out_shape=[jax.ShapeDtypeStruct((bn, nq, WIDTH), BF16)] * 2,
        compiler_params=_params("parallel", "arbitrary"),
        name="sample_b",
    )(page_table, pa, pb, pa, pb, *([c_va] * pp), *([c_vb] * pp), va_new, vb_new)


def _merge_ffn_kernel(x_ref, ya_ref, yb_ref, ga_ref, gb_ref, wa_ref, wb_ref, wo_ref, gf_ref,
                      wg_ref, wu_ref, wd_ref, gfin_ref, y_ref, *, ff_slab):
    za = _dot(ya_ref[...], wa_ref[...])
    zb = _dot(yb_ref[...], wb_ref[...])
    mix = jax.nn.sigmoid(ga_ref[...]) * za + jax.nn.sigmoid(gb_ref[...]) * zb
    x1 = x_ref[...] + _dot(mix.astype(BF16), wo_ref[...])
    ms = jnp.mean(x1 * x1, axis=-1, keepdims=True)
    xn = (x1 * lax.rsqrt(ms + RMS_EPS) * gf_ref[...]).astype(BF16)
    x2 = x1
    d_ff = wg_ref.shape[1]
    for s0 in range(0, d_ff, ff_slab):
        hg = _dot(xn, wg_ref[:, s0:s0 + ff_slab])
        hu = _dot(xn, wu_ref[:, s0:s0 + ff_slab])
        act = (hg * jax.nn.sigmoid(hg) * hu).astype(BF16)
        x2 = x2 + _dot(act, wd_ref[s0:s0 + ff_slab, :])
    ms2 = jnp.mean(x2 * x2, axis=-1, keepdims=True)
    y_ref[...] = x2 * lax.rsqrt(ms2 + RMS_EPS) * gfin_ref[...]


def _merge_ffn_call(x, ya, yb, ga, gb, wa, wb, wo, gf, wg, wu, wd, gfin, tm):
    m, d = x.shape
    d_ff = wg.shape[1]
    ff_slab = d_ff // 2 if (d_ff // 2) % LANES == 0 else d_ff
    row = lambda w: pl.BlockSpec((tm, w), lambda i: (i, 0))
    kern = functools.partial(_merge_ffn_kernel, ff_slab=ff_slab)
    return pl.pallas_call(
        kern,
        grid=(m // tm,),
        in_specs=[row(d), row(WIDTH), row(WIDTH), row(d), row(d),
                  _const_spec(wa.shape), _const_spec(wb.shape), _const_spec(wo.shape), _const_spec((1, d)),
                  _const_spec(wg.shape), _const_spec(wu.shape), _const_spec(wd.shape), _const_spec((1, d))],
        out_specs=row(d),
        out_shape=jax.ShapeDtypeStruct((m, d), F32),
        compiler_params=_params("parallel"),
        name="merge_ffn",
    )(x, ya, yb, ga, gb, wa, wb, wo, gf.reshape(1, d), wg, wu, wd, gfin.reshape(1, d))


def _layer(x_p, x_s, cache_ka, cache_va, cache_idx_k, cache_kb, cache_vb, page_table,
           g_attn, w_in, g_idx_k, w_proj_a, w_proj_b, w_out, g_ffn, w_gate, w_up, w_down, g_final):
    bn, s, d = x_p.shape
    db, nq, _ = x_s.shape
    n_pages = page_table.shape[1]
    n_pool = cache_ka.shape[0]
    past = n_pages * PAGE_SIZE
    assert s % PROJ_ROWS == 0 and Q_TILE == MOBA_BLOCK and PROJ_ROWS % MOBA_BLOCK == 0

    w_arr = _arrange_w_in(w_in)
    gk = jnp.pad(g_idx_k, (0, LANES - IDX_DIM)).reshape(1, LANES)
    wa, wb, wo = w_proj_a.astype(BF16), w_proj_b.astype(BF16), w_out.astype(BF16)
    wg, wu, wd = w_gate.astype(BF16), w_up.astype(BF16), w_down.astype(BF16)

    cs_p, sn_p = _rope_tables(jnp.arange(s))
    (qat, qit, qbt, kab, kbb, vat4, vbt4, kat, vat, kbt, vbt, ga, gb, kwb, kwt, kmean) = _proj_prompt_call(
        x_p.reshape(bn * s, d), g_attn, w_arr, cs_p, sn_p, gk, bn, s)
    ya = _dsa_prompt_call(qat, qit, kwt, kwb, kab, vat4, bn, s)
    nb = s // MOBA_BLOCK
    kmean_b = jnp.pad(kmean.reshape(bn, nb, WIDTH), ((0, 0), (0, -nb % SUBLANES), (0, 0)))
    yb = _moba_prompt_call(qbt, kmean_b, kbb, vbt4, bn, s)
    y_p = _merge_ffn_call(x_p.reshape(bn * s, d), ya, yb, ga, gb, wa, wb, wo, g_ffn, wg, wu, wd, g_final,
                          FFN_ROWS)
    heads_out = lambda t: t.reshape(bn, N_HEADS, HEAD_DIM, s).transpose(0, 3, 1, 2)
    new_p = (heads_out(kat), heads_out(vat), kwt[:, :IDX_DIM, :].transpose(0, 2, 1), heads_out(kbt), heads_out(vbt))

    ms_ = db * nq
    cs_s, sn_s = _rope_tables(past + jnp.arange(nq))
    cs_s, sn_s = jnp.tile(cs_s, (db, 1)), jnp.tile(sn_s, (db, 1))
    (qa_s, qi_s, qb_s, kaf_s, vaf_s, kbf_s, vbf_s, kab_s, vab_s, kbb_s, vbb_s, ga_s, gb_s, kwf_s, kwb_s) = \
        _proj_sample_call(x_s.reshape(ms_, d), g_attn, w_arr, cs_s, sn_s, gk)
    heads = lambda t: t.reshape(db, nq, N_HEADS, HEAD_DIM).transpose(0, 2, 1, 3)
    eye = jnp.eye(N_HEADS, dtype=BF16)
    blockdiag = lambda t: jnp.einsum("bhqd,hg->bhqgd", heads(t), eye).reshape(db, N_HEADS * nq, WIDTH)
    qi_rows = heads(qi_s).reshape(db, N_HEADS * nq, IDX_DIM)
    wi_s = kwf_s[:, IDX_DIM:IDX_DIM + N_IDX_HEADS].reshape(db, nq, N_IDX_HEADS).transpose(0, 2, 1)
    wcol = jnp.broadcast_to(wi_s.reshape(db, N_IDX_HEADS * nq, 1), (db, N_IDX_HEADS * nq, LANES))
    new_t = lambda t: jnp.pad(t.reshape(db, nq, t.shape[-1]).transpose(0, 2, 1), ((0, 0), (0, 0), (0, LANES - nq)))
    page_t = lambda c: c.transpose(0, 2, 3, 1).reshape(n_pool, WIDTH, PAGE_SIZE)
    pa, pb = _sample_a_call(
        page_table, qi_rows, wcol, blockdiag(qa_s), blockdiag(qb_s),
        cache_idx_k.transpose(0, 2, 1), page_t(cache_ka), page_t(cache_kb),
        new_t(kwb_s[:, :IDX_DIM]), new_t(kab_s), new_t(kbb_s), nq)
    ya_s, yb_s = _sample_b_call(page_table, pa, pb, page_t(cache_va), page_t(cache_vb),
                                new_t(vab_s), new_t(vbb_s), nq)
    y_s = _merge_ffn_call(x_s.reshape(ms_, d), ya_s.reshape(ms_, WIDTH), yb_s.reshape(ms_, WIDTH), ga_s, gb_s,
                          wa, wb, wo, g_ffn, wg, wu, wd, g_final, ms_)
    new_s = (kaf_s.reshape(db, nq, N_HEADS, HEAD_DIM), vaf_s.reshape(db, nq, N_HEADS, HEAD_DIM),
             kwf_s[:, :IDX_DIM].reshape(db, nq, IDX_DIM),
             kbf_s.reshape(db, nq, N_HEADS, HEAD_DIM), vbf_s.reshape(db, nq, N_HEADS, HEAD_DIM))
    return y_p.reshape(bn, s, d), y_s.reshape(db, nq, d), new_p, new_s


def kernel(x_prompt, x_sample, cache_ka, cache_va, cache_idx_k, cache_kb, cache_vb, page_table,
           g_attn, w_in, g_idx_k, w_proj_a, w_proj_b, w_out, g_ffn, w_gate, w_up, w_down, g_final):
    assert w_in.shape[0] == 1, "the fused final norm assumes a single layer"
    y_p, y_s, new_p, new_s = _layer(
        x_prompt, x_sample, cache_ka[0], cache_va[0], cache_idx_k[0], cache_kb[0], cache_vb[0], page_table,
        g_attn[0], w_in[0], g_idx_k[0], w_proj_a[0], w_proj_b[0], w_out[0], g_ffn[0],
        w_gate[0], w_up[0], w_down[0], g_final)
    return (y_p, y_s) + tuple(t[None] for t in new_p) + tuple(t[None] for t in new_s)
```

```python
import functools

import jax
import jax.numpy as jnp
from jax import lax
from jax.experimental import pallas as pl
from jax.experimental.pallas import tpu as pltpu

F32 = jnp.float32
BF16 = jnp.bfloat16
I32 = jnp.int32

HEAD_DIM = 64
N_HEADS = 8
WIDTH = N_HEADS * HEAD_DIM
IDX_DIM = 64
N_IDX_HEADS = 8
TOPK_KEYS = 256
MOBA_BLOCK = 256
MOBA_TOPK = 3
PAGE_SIZE = 128
ROPE_THETA = 10000.0
RMS_EPS = 1e-6
NEG = -1e30
LOG2_E = 1.4426950408889634
LANES = 128
SUBLANES = 8
INT_MIN = -(2 ** 31)
VMEM_LIMIT = 56 * 1024 * 1024
Q_TILE = 256
DSA_CHUNK = 256
PROJ_ROWS = 512
FFN_ROWS = 512
PAGES_PER_STEP = 32

_NT = (((1,), (1,)), ((), ()))


def _dot_nt(a, b):
    return lax.dot_general(a, b, _NT, preferred_element_type=F32)


def _dot(a, b):
    return jnp.dot(a, b, preferred_element_type=F32)


def _head(h):
    return slice(h * HEAD_DIM, (h + 1) * HEAD_DIM)


def _params(*sem):
    return pltpu.CompilerParams(dimension_semantics=sem, vmem_limit_bytes=VMEM_LIMIT)


def _const_spec(shape):
    nd = len(shape)
    return pl.BlockSpec(shape, lambda *_: (0,) * nd, pipeline_mode=pl.Buffered(1))


def _rope_tables(pos):
    inv_freq = ROPE_THETA ** (-jnp.arange(0, HEAD_DIM, 2, dtype=F32) / HEAD_DIM)
    ang = pos.astype(F32)[:, None] * inv_freq[None, :]
    cos, sin = jnp.cos(ang), jnp.sin(ang)
    return (jnp.concatenate([cos, cos, cos, cos], axis=-1),
            jnp.concatenate([-sin, sin, -sin, sin], axis=-1))


def _proj_kernel(x_ref, g_ref, w_ref, cs_ref, sn_ref, gk_ref, *outs, transposed, va_tile, vb_tile):
    tm = x_ref.shape[0]
    x = x_ref[...]
    ms = jnp.mean(x * x, axis=-1, keepdims=True)
    xn = (x * lax.rsqrt(ms + RMS_EPS) * g_ref[...]).astype(BF16)
    cs = cs_ref[...]
    sn = sn_ref[...]
    lane = lax.broadcasted_iota(I32, (tm, LANES), 1)
    first_half = (lane & (HEAD_DIM // 2)) == 0

    def rope(p):
        swapped = jnp.where(first_half, pltpu.roll(p, LANES - HEAD_DIM // 2, 1),
                            pltpu.roll(p, HEAD_DIM // 2, 1))
        return p * cs + swapped * sn

    def slab(j, width=WIDTH):
        return _dot(xn, w_ref[:, j * WIDTH:j * WIDTH + width])

    def rope_slab(h):
        return jnp.concatenate(
            [rope(h[:, k * LANES:(k + 1) * LANES]) for k in range(WIDTH // LANES)], axis=1)

    qa, qi, qb = rope_slab(slab(0)), rope_slab(slab(1)), rope_slab(slab(2))
    ka, va, kb, vb = rope_slab(slab(3)), slab(4), rope_slab(slab(5)), slab(6)
    t = slab(11, LANES)
    is_key = lane < IDX_DIM
    kms = jnp.sum(jnp.where(is_key, t * t, 0.0), axis=-1, keepdims=True) * (1.0 / IDX_DIM)
    kn = rope(t * lax.rsqrt(kms + RMS_EPS) * gk_ref[...])
    kw = jnp.where(is_key, kn, jnp.where(lane < IDX_DIM + N_IDX_HEADS, t * (N_IDX_HEADS ** -0.5), 0.0))

    if transposed:
        (qat_ref, qit_ref, qbt_ref, kab_ref, kbb_ref, vat4_ref, vbt4_ref,
         kat_ref, vat_ref, kbt_ref, vbt_ref, ga_ref, gb_ref, kwb_ref, kwt_ref, kmean_ref) = outs
        qat_ref[0] = qa.T.astype(BF16)
        qit_ref[0] = qi.T.astype(BF16)
        qbt_ref[0] = qb.T.astype(BF16)
        kab_ref[...] = ka.astype(BF16)
        kbb_ref[...] = kb.astype(BF16)
        kat_ref[0] = ka.T
        kbt_ref[0] = kb.T
        va_t, vb_t = va.T, vb.T
        vat_ref[0] = va_t
        vbt_ref[0] = vb_t
        for r in range(tm // va_tile):
            vat4_ref[0, r] = va_t[:, r * va_tile:(r + 1) * va_tile].astype(BF16)
        for r in range(tm // vb_tile):
            vbt4_ref[0, r] = vb_t[:, r * vb_tile:(r + 1) * vb_tile].astype(BF16)
        kwb_ref[...] = kw.astype(BF16)
        kwt_ref[0] = kw.T
        for r in range(tm // MOBA_BLOCK):
            kmean_ref[0, r:r + 1, :] = jnp.mean(kb[r * MOBA_BLOCK:(r + 1) * MOBA_BLOCK], axis=0, keepdims=True)
    else:
        (qa_ref, qi_ref, qb_ref, kaf_ref, vaf_ref, kbf_ref, vbf_ref,
         kab_ref, vab_ref, kbb_ref, vbb_ref, ga_ref, gb_ref, kwf_ref, kwb_ref) = outs
        qa_ref[...] = qa.astype(BF16)
        qi_ref[...] = qi.astype(BF16)
        qb_ref[...] = qb.astype(BF16)
        for f_ref, b_ref, val in ((kaf_ref, kab_ref, ka), (vaf_ref, vab_ref, va),
                                  (kbf_ref, kbb_ref, kb), (vbf_ref, vbb_ref, vb)):
            f_ref[...] = val
            b_ref[...] = val.astype(BF16)
        kwf_ref[...] = kw
        kwb_ref[...] = kw.astype(BF16)
    ga_ref[:, :WIDTH] = slab(7)
    ga_ref[:, WIDTH:] = slab(8)
    gb_ref[:, :WIDTH] = slab(9)
    gb_ref[:, WIDTH:] = slab(10)


def _arrange_w_in(w_in):
    d = w_in.shape[0]
    o = [0]
    for s in (WIDTH, WIDTH, WIDTH, N_IDX_HEADS * IDX_DIM, IDX_DIM, N_IDX_HEADS, WIDTH, WIDTH, WIDTH, d, d):
        o.append(o[-1] + s)
    qa, ka, va, qi, ki, wi, qb, kb, vb, ga, gb = [w_in[:, o[i]:o[i + 1]] for i in range(11)]
    qscale = HEAD_DIM ** -0.5 * LOG2_E
    iscale = IDX_DIM ** -0.5
    pad = jnp.zeros((d, LANES - IDX_DIM - N_IDX_HEADS), w_in.dtype)
    return jnp.concatenate([qa * qscale, qi * iscale, qb * qscale, ka, va, kb, vb, ga, gb, ki, wi, pad],
                           axis=1).astype(BF16)


def _proj_prompt_call(x, g_attn, w_arr, cs, sn, gk, bn, s):
    m, d = x.shape
    tm = PROJ_ROWS
    spt = s // tm
    row = lambda w: pl.BlockSpec((tm, w), lambda i: (i, 0))
    tab = pl.BlockSpec((tm, LANES), lambda i: (i % spt, 0))
    col = lambda r: pl.BlockSpec((1, r, tm), lambda i: (i // spt, 0, i % spt))
    tile4 = lambda w: pl.BlockSpec((1, tm // w, WIDTH, w), lambda i: (i // spt, i % spt, 0, 0))
    t_shape = lambda r, dt: jax.ShapeDtypeStruct((bn, r, s), dt)
    out_shape = (
        [t_shape(WIDTH, BF16)] * 3
        + [jax.ShapeDtypeStruct((m, WIDTH), BF16)] * 2
        + [jax.ShapeDtypeStruct((bn, s // DSA_CHUNK, WIDTH, DSA_CHUNK), BF16),
           jax.ShapeDtypeStruct((bn, s // MOBA_BLOCK, WIDTH, MOBA_BLOCK), BF16)]
        + [t_shape(WIDTH, F32)] * 4
        + [jax.ShapeDtypeStruct((m, d), F32)] * 2
        + [jax.ShapeDtypeStruct((m, LANES), BF16), t_shape(LANES, F32),
           jax.ShapeDtypeStruct((m // tm, tm // MOBA_BLOCK, WIDTH), F32)])
    out_specs = ([col(WIDTH)] * 3 + [row(WIDTH)] * 2 + [tile4(DSA_CHUNK), tile4(MOBA_BLOCK)]
                 + [col(WIDTH)] * 4 + [row(d)] * 2 + [row(LANES), col(LANES),
                 pl.BlockSpec((1, tm // MOBA_BLOCK, WIDTH), lambda i: (i, 0, 0))])
    kern = functools.partial(_proj_kernel, transposed=True, va_tile=DSA_CHUNK, vb_tile=MOBA_BLOCK)
    return pl.pallas_call(
        kern,
        grid=(m // tm,),
        in_specs=[row(d), _const_spec((1, d)), _const_spec(w_arr.shape), tab, tab, _const_spec((1, LANES))],
        out_specs=out_specs,
        out_shape=out_shape,
        compiler_params=_params("parallel"),
        name="proj",
    )(x, g_attn.reshape(1, d), w_arr, cs, sn, gk)


def _proj_sample_call(x, g_attn, w_arr, cs, sn, gk):
    m, d = x.shape
    row = lambda w: pl.BlockSpec((m, w), lambda i: (0, 0))
    out_shape = ([jax.ShapeDtypeStruct((m, WIDTH), BF16)] * 3 + [jax.ShapeDtypeStruct((m, WIDTH), F32)] * 4
                 + [jax.ShapeDtypeStruct((m, WIDTH), BF16)] * 4 + [jax.ShapeDtypeStruct((m, d), F32)] * 2
                 + [jax.ShapeDtypeStruct((m, LANES), F32), jax.ShapeDtypeStruct((m, LANES), BF16)])
    kern = functools.partial(_proj_kernel, transposed=False, va_tile=0, vb_tile=0)
    return pl.pallas_call(
        kern,
        grid=(1,),
        in_specs=[row(d), _const_spec((1, d)), _const_spec(w_arr.shape), row(LANES), row(LANES),
                  _const_spec((1, LANES))],
        out_specs=[row(WIDTH)] * 11 + [row(d)] * 2 + [row(LANES)] * 2,
        out_shape=out_shape,
        compiler_params=_params("arbitrary"),
        name="proj_sample",
    )(x, g_attn.reshape(1, d), w_arr, cs, sn, gk)


def _key_to_float(key):
    return lax.bitcast_convert_type(key ^ ((key >> 31) & jnp.int32(0x7FFFFFFF)), F32)


def _select_threshold(count, state_shape, bc, k_sel, idx_bits, y_ref, unroll=False):
    kf = float(k_sel)
    rep = lambda v: jnp.broadcast_to(v, state_shape)

    def bis(t, key):
        cand = key + lax.shift_left(jnp.int32(1), 31 - t)
        cf = bc(_key_to_float(cand))
        return jnp.where(rep(count(lambda x, _: x >= cf)) >= kf, cand, key)

    thr = _key_to_float(lax.fori_loop(0, 32, bis, jnp.full(state_shape, INT_MIN, I32), unroll=unroll))
    tb = bc(thr)
    c_ge = count(lambda x, _: x >= tb)
    y_ref[...] = jnp.full(state_shape, 2 ** 30, I32)

    @pl.when(jnp.max(c_ge) > kf)
    def _():
        need = rep(kf - count(lambda x, _: x > tb))

        def ybis(t, y):
            cand = y + lax.shift_left(jnp.int32(1), idx_bits - 1 - t)
            cb = bc(cand)
            f = count(lambda x, idx: (x == tb) & (idx < cb))
            return jnp.where(rep(f) < need, cand, y)

        y_ref[...] = lax.fori_loop(0, idx_bits, ybis, jnp.zeros(state_shape, I32))

    return thr


def _top_blocks(gate, valid, n_top, axis):
    n = gate.shape[axis]
    ids = lax.broadcasted_iota(I32, gate.shape, axis).astype(F32)
    g = jnp.where(valid, gate, NEG)
    marks = jnp.zeros(gate.shape, F32)
    for _ in range(n_top):
        mx = jnp.max(g, axis=axis, keepdims=True)
        first = jnp.min(jnp.where(g == mx, ids, float(n)), axis=axis, keepdims=True)
        hit = ids == first
        marks = jnp.where(hit, 1.0, marks)
        g = jnp.where(hit, -jnp.inf, g)
    return jnp.where(valid, marks, 0.0)


def _attend_chunk_t(kc, qt_ref, vt, bias, pick, lg_ref, p_ref, m_ref, l_ref, acc_ref):
    for h in range(N_HEADS):
        lg = _dot(kc[:, _head(h)], qt_ref[0, _head(h), :])
        lg_ref[h] = lg if bias is None else lg + bias(h)
    for h in range(N_HEADS):
        lg = lg_ref[h]
        m_old = m_ref[h]
        mx = jnp.max(lg, axis=0, keepdims=True)
        if pick is None:
            m_new = jnp.maximum(m_old, mx)
            shift = m_new[0:1]
        else:
            m_new = jnp.maximum(m_old, jnp.where(pick(h), mx, NEG))
            shift = jnp.where(pick(h), m_new[0:1], -NEG)
        alpha = jnp.exp2(m_old - m_new)
        p = jnp.exp2(lg - shift)
        l_ref[h] = alpha * l_ref[h] + jnp.sum(p, axis=0, keepdims=True)
        m_ref[h] = m_new
        p_ref[h] = p.astype(BF16)
        acc_ref[_head(h), :] = alpha[0:1] * acc_ref[_head(h), :]
    for h in range(N_HEADS):
        acc_ref[_head(h), :] += _dot(vt(h), p_ref[h])


def _init_softmax_state(m_ref, l_ref, acc_ref):
    m_ref[...] = jnp.full(m_ref.shape, NEG, F32)
    l_ref[...] = jnp.zeros(l_ref.shape, F32)
    acc_ref[...] = jnp.zeros(acc_ref.shape, F32)


def _finish_heads_t(o_ref, l_ref, acc_ref):
    for h in range(N_HEADS):
        acc_ref[_head(h), :] = acc_ref[_head(h), :] / l_ref[h][0:1]
    o_ref[...] = acc_ref[...].T.astype(o_ref.dtype)


def _attn_scratch(c_w, qt):
    return [pltpu.VMEM((N_HEADS, c_w, qt), F32), pltpu.VMEM((N_HEADS, c_w, qt), BF16),
            pltpu.VMEM((N_HEADS, SUBLANES, qt), F32), pltpu.VMEM((N_HEADS, SUBLANES, qt), F32),
            pltpu.VMEM((WIDTH, qt), F32)]


def _dsa_prompt_kernel(qat_ref, qit_ref, kwt_ref, kw_ref, ka_ref, vt_ref, o_ref,
                       sc_ref, y_ref, lg_ref, p_ref, m_ref, l_ref, acc_ref, *, k_sel, idx_bits):
    i = pl.program_id(1)
    _, c_w, qt = sc_ref.shape
    nkc = ((i + 1) * qt + c_w - 1) // c_w
    row = lax.broadcasted_iota(I32, (c_w, qt), 0)
    qpos = i * qt + lax.broadcasted_iota(I32, (c_w, qt), 1)

    def score_body(c, carry):
        k0 = pl.multiple_of(c * c_w, c_w)
        ki = kw_ref[pl.ds(k0, c_w), :][:, :IDX_DIM]
        acc = jnp.zeros((c_w, qt), F32)
        for h in range(N_IDX_HEADS):
            s = _dot(ki, qit_ref[0, h * IDX_DIM:(h + 1) * IDX_DIM, :])
            acc = acc + jnp.maximum(s, 0.0) * kwt_ref[0, IDX_DIM + h:IDX_DIM + h + 1, :]
        sc_ref[c] = jnp.where((k0 + row) <= qpos, acc, NEG)
        return carry

    lax.fori_loop(0, nkc, score_body, 0)

    @pl.when(nkc % 2 == 1)
    def _():
        sc_ref[nkc] = jnp.full((c_w, qt), NEG, F32)

    def count(pred):
        def body(c2, cnt):
            for c in (2 * c2, 2 * c2 + 1):
                hit = jnp.where(pred(sc_ref[c], c * c_w + row), 1.0, 0.0)
                cnt = cnt + jnp.sum(hit.reshape(c_w // SUBLANES, SUBLANES, qt), axis=0)
            return cnt

        cnt = lax.fori_loop(0, (nkc + 1) // 2, body, jnp.zeros((SUBLANES, qt), F32))
        return jnp.sum(cnt, axis=0, keepdims=True)

    thr = _select_threshold(count, (SUBLANES, qt), lambda v: v[0:1], k_sel, idx_bits, y_ref)
    tb = thr[0:1]
    yb = y_ref[0:1, :]
    _init_softmax_state(m_ref, l_ref, acc_ref)

    def att_body(c, carry):
        k0 = pl.multiple_of(c * c_w, c_w)
        x = sc_ref[c]
        kidx = k0 + row
        sel = ((x > tb) | ((x == tb) & (kidx <= yb))) & (kidx <= qpos)
        bias = jnp.where(sel, 0.0, NEG)
        _attend_chunk_t(ka_ref[pl.ds(k0, c_w), :], qat_ref, lambda h: vt_ref[0, c, _head(h), :],
                        lambda h: bias, None, lg_ref, p_ref, m_ref, l_ref, acc_ref)
        return carry

    lax.fori_loop(0, nkc, att_body, 0)
    _finish_heads_t(o_ref, l_ref, acc_ref)


def _dsa_prompt_call(qat, qit, kwt, kwb, kab, vat4, bn, s):
    qt, c_w = Q_TILE, DSA_CHUNK
    assert (s // c_w) % 2 == 0
    nq = s // qt
    k_sel = min(TOPK_KEYS, s // 4)
    idx_bits = int(s).bit_length()
    qcol = lambda r: pl.BlockSpec((1, r, qt), lambda b, i: (b, 0, i))
    seq = lambda w: pl.BlockSpec((s, w), lambda b, i: (b, 0))
    kern = functools.partial(_dsa_prompt_kernel, k_sel=k_sel, idx_bits=idx_bits)
    return pl.pallas_call(
        kern,
        grid=(bn, nq),
        in_specs=[qcol(WIDTH), qcol(WIDTH), qcol(LANES), seq(LANES), seq(WIDTH),
                  pl.BlockSpec((1, s // c_w, WIDTH, c_w), lambda b, i: (b, 0, 0, 0))],
        out_specs=pl.BlockSpec((qt, WIDTH), lambda b, i: (b * nq + i, 0)),
        out_shape=jax.ShapeDtypeStruct((bn * s, WIDTH), BF16),
        scratch_shapes=[pltpu.VMEM((s // c_w, c_w, qt), F32), pltpu.VMEM((SUBLANES, qt), I32)]
        + _attn_scratch(c_w, qt),
        compiler_params=_params("parallel", "arbitrary"),
        name="dsa_prompt",
    )(qat, qit, kwt, kwb, kab, vat4)


def _moba_prompt_kernel(qbt_ref, km_ref, kb_ref, vt_ref, o_ref, marks_ref, lg_ref, p_ref, m_ref, l_ref, acc_ref):
    cur = pl.program_id(1)
    nb, qt = marks_ref.shape[1:]
    km = km_ref[0].astype(BF16)
    past = lax.broadcasted_iota(I32, (nb, qt), 0) < cur
    for h in range(N_HEADS):
        marks_ref[h] = _top_blocks(_dot(km[:, _head(h)], qbt_ref[0, _head(h), :]), past, MOBA_TOPK, 0)
    _init_softmax_state(m_ref, l_ref, acc_ref)

    def block(j, bias, pick):
        kc = kb_ref[pl.ds(pl.multiple_of(j * MOBA_BLOCK, MOBA_BLOCK), MOBA_BLOCK), :]
        _attend_chunk_t(kc, qbt_ref, lambda h: vt_ref[0, j, _head(h), :], bias, pick,
                        lg_ref, p_ref, m_ref, l_ref, acc_ref)

    def blk_body(j, carry):
        block(j, None, lambda h: marks_ref[h, pl.ds(j, 1), :] > 0.5)
        return carry

    lax.fori_loop(0, cur, blk_body, 0)
    causal = (lax.broadcasted_iota(I32, (MOBA_BLOCK, qt), 0) <= lax.broadcasted_iota(I32, (MOBA_BLOCK, qt), 1))
    own_bias = jnp.where(causal, 0.0, NEG)
    block(cur, lambda h: own_bias, None)
    _finish_heads_t(o_ref, l_ref, acc_ref)


def _moba_prompt_call(qbt, kmean, kbb, vbt4, bn, s):
    qt = MOBA_BLOCK
    nq = s // qt
    nb = kmean.shape[1]
    return pl.pallas_call(
        _moba_prompt_kernel,
        grid=(bn, nq),
        in_specs=[pl.BlockSpec((1, WIDTH, qt), lambda b, i: (b, 0, i)),
                  pl.BlockSpec((1, nb, WIDTH), lambda b, i: (b, 0, 0)),
                  pl.BlockSpec((s, WIDTH), lambda b, i: (b, 0)),
                  pl.BlockSpec((1, nq, WIDTH, qt), lambda b, i: (b, 0, 0, 0))],
        out_specs=pl.BlockSpec((qt, WIDTH), lambda b, i: (b * nq + i, 0)),
        out_shape=jax.ShapeDtypeStruct((bn * s, WIDTH), BF16),
        scratch_shapes=[pltpu.VMEM((N_HEADS, nb, qt), F32)] + _attn_scratch(MOBA_BLOCK, qt),
        compiler_params=_params("parallel", "arbitrary"),
        name="moba_prompt",
    )(qbt, kmean, kbb, vbt4)


def _sample_a_kernel(pt_ref, qi_ref, wcol_ref, qda_ref, qdb_ref, *refs, n_pages, nq, k_sel, idx_bits):
    del pt_ref
    pp = PAGES_PER_STEP
    idx_refs, ka_refs, kb_refs = refs[0:pp], refs[pp:2 * pp], refs[2 * pp:3 * pp]
    kin_ref, kan_ref, kbn_ref, pa_ref, pb_ref, sc_ref, y_ref, la_ref, lb_ref = refs[3 * pp:]
    p = pl.program_id(1)
    nch = n_pages + 1
    rows = N_HEADS * nq
    lane = lax.broadcasted_iota(I32, (rows, LANES), 1)
    qrow = lax.broadcasted_iota(I32, (rows, LANES), 0) % nq

    def idx_scores(k_t):
        s = jnp.maximum(_dot(qi_ref[0], k_t), 0.0) * wcol_ref[0]
        sc = s[0:nq]
        for h in range(1, N_IDX_HEADS):
            sc = sc + s[h * nq:(h + 1) * nq]
        return sc

    for t in range(pp):
        g = p * pp + t
        sc_ref[g] = idx_scores(idx_refs[t][0].astype(BF16))
        la_ref[g] = _dot(qda_ref[0], ka_refs[t][0].astype(BF16))
        lb_ref[g] = _dot(qdb_ref[0], kb_refs[t][0].astype(BF16))

    def softmax_out(l_ref, p_ref, keep):
        lg = jnp.where(keep, l_ref[...].reshape(nch, N_HEADS, nq, LANES), NEG)
        m = jnp.max(jnp.max(lg, axis=0), axis=-1, keepdims=True)
        e = jnp.exp2(lg - m)
        inv = 1.0 / jnp.sum(jnp.sum(e, axis=0), axis=-1, keepdims=True)
        p_ref[0] = (e * inv).reshape(nch, rows, LANES).astype(p_ref.dtype)

    @pl.when(p == n_pages // pp - 1)
    def _():
        new_ok = (lane <= qrow) & (lane < nq)
        kidx = (lax.broadcasted_iota(I32, (nch, nq, LANES), 0) * LANES
                + lax.broadcasted_iota(I32, (nch, nq, LANES), 2))
        cpb = MOBA_BLOCK // PAGE_SIZE
        n_blk = n_pages // cpb
        lb_ref[n_pages] = jnp.where(new_ok, _dot(qdb_ref[0], kbn_ref[0]), NEG)
        gate = jnp.zeros((rows, LANES), F32)
        for j in range(n_blk):
            tot = lb_ref[j * cpb]
            for c in range(1, cpb):
                tot = tot + lb_ref[j * cpb + c]
            gate = jnp.where(lane == j, jnp.sum(tot, axis=-1, keepdims=True) * (1.0 / MOBA_BLOCK), gate)
        marks = _top_blocks(gate, lane < n_blk, MOBA_TOPK, 1)
        cols = [jnp.sum(jnp.where(lane == j, marks, 0.0), axis=-1, keepdims=True) for j in range(n_blk)]
        keep = jnp.stack([jnp.broadcast_to(cols[c // cpb], (rows, LANES)) for c in range(n_pages)]
                         + [jnp.ones((rows, LANES), F32)], axis=0)
        softmax_out(lb_ref, pb_ref, keep.reshape(nch, N_HEADS, nq, LANES) > 0.5)
        sc_ref[n_pages] = jnp.where(new_ok[:nq], idx_scores(kin_ref[0]), NEG)
        sc = sc_ref[...]

        def count(pred):
            return jnp.sum(jnp.sum(jnp.where(pred(sc, kidx), 1.0, 0.0), axis=0), axis=-1, keepdims=True)

        thr = _select_threshold(count, (nq, LANES), lambda v: v, k_sel, idx_bits, y_ref, unroll=True)
        sel = (sc > thr) | ((sc == thr) & (kidx <= y_ref[...]))
        la_ref[n_pages] = jnp.where(new_ok, _dot(qda_ref[0], kan_ref[0]), NEG)
        softmax_out(la_ref, pa_ref, sel[:, None])


def _page_specs(rows):
    pp = PAGES_PER_STEP
    return [pl.BlockSpec((1, rows, PAGE_SIZE), functools.partial(
        lambda b, p, pt, t: (pt[b, p * pp + t], 0, 0), t=t)) for t in range(pp)]


def _sample_a_call(page_table, qi_rows, wcol, qda, qdb, c_idx, c_ka, c_kb, ki_new, ka_new, kb_new, nq):
    bn, n_pages = page_table.shape
    pp = PAGES_PER_STEP
    assert n_pages % pp == 0 and MOBA_BLOCK % PAGE_SIZE == 0
    rows = N_HEADS * nq
    nch = n_pages + 1
    k_sel = min(TOPK_KEYS, (n_pages * PAGE_SIZE + nq) // 4)
    idx_bits = int(nch * LANES).bit_length()
    per_b = lambda shp: pl.BlockSpec((1,) + shp, lambda b, p, pt: (b,) + (0,) * len(shp))
    kern = functools.partial(_sample_a_kernel, n_pages=n_pages, nq=nq, k_sel=k_sel, idx_bits=idx_bits)
    grid_spec = pltpu.PrefetchScalarGridSpec(
        num_scalar_prefetch=1,
        grid=(bn, n_pages // pp),
        in_specs=[per_b((rows, IDX_DIM)), per_b((rows, LANES)), per_b((rows, WIDTH)), per_b((rows, WIDTH))]
        + _page_specs(IDX_DIM) + _page_specs(WIDTH) + _page_specs(WIDTH)
        + [per_b((IDX_DIM, LANES)), per_b((WIDTH, LANES)), per_b((WIDTH, LANES))],
        out_specs=[per_b((nch, rows, LANES)), per_b((nch, rows, LANES))],
        scratch_shapes=[
            pltpu.VMEM((nch, nq, LANES), F32),
            pltpu.VMEM((nq, LANES), I32),
            pltpu.VMEM((nch, rows, LANES), F32),
            pltpu.VMEM((nch, rows, LANES), F32),
        ],
    )
    return pl.pallas_call(
        kern,
        grid_spec=grid_spec,
        out_shape=[jax.ShapeDtypeStruct((bn, nch, rows, LANES), BF16)] * 2,
        compiler_params=_params("parallel", "arbitrary"),
        name="sample_a",
    )(page_table, qi_rows, wcol, qda, qdb, *([c_idx] * pp), *([c_ka] * pp), *([c_kb] * pp),
      ki_new, ka_new, kb_new)


def _sample_b_kernel(pt_ref, pa_ref, pb_ref, pan_ref, pbn_ref, *refs, n_pages, nq):
    del pt_ref
    pp = PAGES_PER_STEP
    va_refs, vb_refs = refs[0:pp], refs[pp:2 * pp]
    van_ref, vbn_ref, oa_ref, ob_ref, acca_ref, accb_ref = refs[2 * pp:]
    p = pl.program_id(1)
    rows = N_HEADS * nq

    @pl.when(p == 0)
    def _():
        acca_ref[...] = _dot_nt(van_ref[0], pan_ref[0, 0])
        accb_ref[...] = _dot_nt(vbn_ref[0], pbn_ref[0, 0])

    for v_refs, p_ref, acc_ref in ((va_refs, pa_ref, acca_ref), (vb_refs, pb_ref, accb_ref)):
        v_all = jnp.concatenate([r[0].astype(BF16) for r in v_refs], axis=1)
        p_all = jnp.concatenate([p_ref[0, t] for t in range(pp)], axis=1)
        acc_ref[...] += _dot_nt(v_all, p_all)

    @pl.when(p == n_pages // pp - 1)
    def _():
        same_head = (lax.broadcasted_iota(I32, (WIDTH, rows), 0) // HEAD_DIM
                     == lax.broadcasted_iota(I32, (WIDTH, rows), 1) // nq)
        pick_q = jnp.where(lax.broadcasted_iota(I32, (nq, rows), 1) % nq
                           == lax.broadcasted_iota(I32, (nq, rows), 0), 1.0, 0.0).astype(BF16)
        for acc_ref, o_ref in ((acca_ref, oa_ref), (accb_ref, ob_ref)):
            own = jnp.where(same_head, acc_ref[...], 0.0).astype(o_ref.dtype)
            o_ref[0] = _dot_nt(pick_q, own).astype(o_ref.dtype)


def _sample_b_call(page_table, pa, pb, c_va, c_vb, va_new, vb_new, nq):
    bn, n_pages = page_table.shape
    pp = PAGES_PER_STEP
    rows = N_HEADS * nq
    per_b = lambda shp: pl.BlockSpec((1,) + shp, lambda b, p, pt: (b,) + (0,) * len(shp))
    pcol = pl.BlockSpec((1, pp, rows, LANES), lambda b, p, pt: (b, p, 0, 0))
    pnew = pl.BlockSpec((1, 1, rows, LANES), lambda b, p, pt: (b, n_pages, 0, 0))
    kern = functools.partial(_sample_b_kernel, n_pages=n_pages, nq=nq)
    grid_spec = pltpu.PrefetchScalarGridSpec(
        num_scalar_prefetch=1,
        grid=(bn, n_pages // pp),
        in_specs=[pcol, pcol, pnew, pnew] + _page_specs(WIDTH) + _page_specs(WIDTH)
        + [per_b((WIDTH, LANES)), per_b((WIDTH, LANES))],
        out_specs=[per_b((nq, WIDTH)), per_b((nq, WIDTH))],
        scratch_shapes=[pltpu.VMEM((WIDTH, rows), F32), pltpu.VMEM((WIDTH, rows), F32)],
    )
    return pl.pallas_call(
        kern,
        grid_spec=grid_spec,
        out_shape=[jax.ShapeDtypeStruct((bn, nq, WIDTH), BF16)] * 2,
        compiler_params=_params("parallel", "arbitrary"),
        name="sample_b",
    )(page_table, pa, pb, pa, pb, *([c_va] * pp), *([c_vb] * pp), va_new, vb_new)


def _merge_ffn_kernel(x_ref, ya_ref, yb_ref, ga_ref, gb_ref, wa_ref, wb_ref, wo_ref, gf_ref,
                      wg_ref, wu_ref, wd_ref, gfin_ref, y_ref, *, ff_slab):
    za = _dot(ya_ref[...], wa_ref[...])
    zb = _dot(yb_ref[...], wb_ref[...])
    mix = jax.nn.sigmoid(ga_ref[...]) * za + jax.nn.sigmoid(gb_ref[...]) * zb
    x1 = x_ref[...] + _dot(mix.astype(BF16), wo_ref[...])
    ms = jnp.mean(x1 * x1, axis=-1, keepdims=True)
    xn = (x1 * lax.rsqrt(ms + RMS_EPS) * gf_ref[...]).astype(BF16)
    x2 = x1
    d_ff = wg_ref.shape[1]
    for s0 in range(0, d_ff, ff_slab):
        hg = _dot(xn, wg_ref[:, s0:s0 + ff_slab])
        hu = _dot(xn, wu_ref[:, s0:s0 + ff_slab])
        act = (hg * jax.nn.sigmoid(hg) * hu).astype(BF16)
        x2 = x2 + _dot(act, wd_ref[s0:s0 + ff_slab, :])
    ms2 = jnp.mean(x2 * x2, axis=-1, keepdims=True)
    y_ref[...] = x2 * lax.rsqrt(ms2 + RMS_EPS) * gfin_ref[...]


def _merge_ffn_call(x, ya, yb, ga, gb, wa, wb, wo, gf, wg, wu, wd, gfin, tm):
    m, d = x.shape
    d_ff = wg.shape[1]
    ff_slab = d_ff // 2 if (d_ff // 2) % LANES == 0 else d_ff
    row = lambda w: pl.BlockSpec((tm, w), lambda i: (i, 0))
    kern = functools.partial(_merge_ffn_kernel, ff_slab=ff_slab)
    return pl.pallas_call(
        kern,
        grid=(m // tm,),
        in_specs=[row(d), row(WIDTH), row(WIDTH), row(d), row(d),
                  _const_spec(wa.shape), _const_spec(wb.shape), _const_spec(wo.shape), _const_spec((1, d)),
                  _const_spec(wg.shape), _const_spec(wu.shape), _const_spec(wd.shape), _const_spec((1, d))],
        out_specs=row(d),
        out_shape=jax.ShapeDtypeStruct((m, d), F32),
        compiler_params=_params("parallel"),
        name="merge_ffn",
    )(x, ya, yb, ga, gb, wa, wb, wo, gf.reshape(1, d), wg, wu, wd, gfin.reshape(1, d))


def _layer(x_p, x_s, cache_ka, cache_va, cache_idx_k, cache_kb, cache_vb, page_table,
           g_attn, w_in, g_idx_k, w_proj_a, w_proj_b, w_out, g_ffn, w_gate, w_up, w_down, g_final):
    bn, s, d = x_p.shape
    db, nq, _ = x_s.shape
    n_pages = page_table.shape[1]
    n_pool = cache_ka.shape[0]
    past = n_pages * PAGE_SIZE
    assert s % PROJ_ROWS == 0 and Q_TILE == MOBA_BLOCK and PROJ_ROWS % MOBA_BLOCK == 0

    w_arr = _arrange_w_in(w_in)
    gk = jnp.pad(g_idx_k, (0, LANES - IDX_DIM)).reshape(1, LANES)
    wa, wb, wo = w_proj_a.astype(BF16), w_proj_b.astype(BF16), w_out.astype(BF16)
    wg, wu, wd = w_gate.astype(BF16), w_up.astype(BF16), w_down.astype(BF16)

    cs_p, sn_p = _rope_tables(jnp.arange(s))
    (qat, qit, qbt, kab, kbb, vat4, vbt4, kat, vat, kbt, vbt, ga, gb, kwb, kwt, kmean) = _proj_prompt_call(
        x_p.reshape(bn * s, d), g_attn, w_arr, cs_p, sn_p, gk, bn, s)
    ya = _dsa_prompt_call(qat, qit, kwt, kwb, kab, vat4, bn, s)
    nb = s // MOBA_BLOCK
    kmean_b = jnp.pad(kmean.reshape(bn, nb, WIDTH), ((0, 0), (0, -nb % SUBLANES), (0, 0)))
    yb = _moba_prompt_call(qbt, kmean_b, kbb, vbt4, bn, s)
    y_p = _merge_ffn_call(x_p.reshape(bn * s, d), ya, yb, ga, gb, wa, wb, wo, g_ffn, wg, wu, wd, g_final,
                          FFN_ROWS)
    heads_out = lambda t: t.reshape(bn, N_HEADS, HEAD_DIM, s).transpose(0, 3, 1, 2)
    new_p = (heads_out(kat), heads_out(vat), kwt[:, :IDX_DIM, :].transpose(0, 2, 1), heads_out(kbt), heads_out(vbt))

    ms_ = db * nq
    cs_s, sn_s = _rope_tables(past + jnp.arange(nq))
    cs_s, sn_s = jnp.tile(cs_s, (db, 1)), jnp.tile(sn_s, (db, 1))
    (qa_s, qi_s, qb_s, kaf_s, vaf_s, kbf_s, vbf_s, kab_s, vab_s, kbb_s, vbb_s, ga_s, gb_s, kwf_s, kwb_s) = \
        _proj_sample_call(x_s.reshape(ms_, d), g_attn, w_arr, cs_s, sn_s, gk)
    heads = lambda t: t.reshape(db, nq, N_HEADS, HEAD_DIM).transpose(0, 2, 1, 3)
    eye = jnp.eye(N_HEADS, dtype=BF16)
    blockdiag = lambda t: jnp.einsum("bhqd,hg->bhqgd", heads(t), eye).reshape(db, N_HEADS * nq, WIDTH)
    qi_rows = heads(qi_s).reshape(db, N_HEADS * nq, IDX_DIM)
    wi_s = kwf_s[:, IDX_DIM:IDX_DIM + N_IDX_HEADS].reshape(db, nq, N_IDX_HEADS).transpose(0, 2, 1)
    wcol = jnp.broadcast_to(wi_s.reshape(db, N_IDX_HEADS * nq, 1), (db, N_IDX_HEADS * nq, LANES))
    new_t = lambda t: jnp.pad(t.reshape(db, nq, t.shape[-1]).transpose(0, 2, 1), ((0, 0), (0, 0), (0, LANES - nq)))
    page_t = lambda c: c.transpose(0, 2, 3, 1).reshape(n_pool, WIDTH, PAGE_SIZE)
    pa, pb = _sample_a_call(
        page_table, qi_rows, wcol, blockdiag(qa_s), blockdiag(qb_s),
        cache_idx_k.transpose(0, 2, 1), page_t(cache_ka), page_t(cache_kb),
        new_t(kwb_s[:, :IDX_DIM]), new_t(kab_s), new_t(kbb_s), nq)
    ya_s, yb_s = _sample_b_call(page_table, pa, pb, page_t(cache_va), page_t(cache_vb),
                                new_t(vab_s), new_t(vbb_s), nq)
    y_s = _merge_ffn_call(x_s.reshape(ms_, d), ya_s.reshape(ms_, WIDTH), yb_s.reshape(ms_, WIDTH), ga_s, gb_s,
                          wa, wb, wo, g_ffn, wg, wu, wd, g_final, ms_)
    new_s = (kaf_s.reshape(db, nq, N_HEADS, HEAD_DIM), vaf_s.reshape(db, nq, N_HEADS, HEAD_DIM),
             kwf_s[:, :IDX_DIM].reshape(db, nq, IDX_DIM),
             kbf_s.reshape(db, nq, N_HEADS, HEAD_DIM), vbf_s.reshape(db, nq, N_HEADS, HEAD_DIM))
    return y_p.reshape(bn, s, d), y_s.reshape(db, nq, d), new_p, new_s


def kernel(x_prompt, x_sample, cache_ka, cache_va, cache_idx_k, cache_kb, cache_vb, page_table,
           g_attn, w_in, g_idx_k, w_proj_a, w_proj_b, w_out, g_ffn, w_gate, w_up, w_down, g_final):
    assert w_in.shape[0] == 1, "the fused final norm assumes a single layer"
    y_p, y_s, new_p, new_s = _layer(
        x_prompt, x_sample, cache_ka[0], cache_va[0], cache_idx_k[0], cache_kb[0], cache_vb[0], page_table,
        g_attn[0], w_in[0], g_idx_k[0], w_proj_a[0], w_proj_b[0], w_out[0], g_ffn[0],
        w_gate[0], w_up[0], w_down[0], g_final)
    return (y_p, y_s) + tuple(t[None] for t in new_p) + tuple(t[None] for t in new_s)
```
